```python
import math
import jax, jax.numpy as jnp
from jax import lax
import numpy as np

D_MODEL = 2048
BATCH = 4
SEQ = 4096
DEPTH = 4

DIFF_HEADS = 8
DIFF_HD = 64
DIFF_VD = 2 * DIFF_HD
DIFF_W = DIFF_HEADS * DIFF_VD
Q_BLOCK = 128
ROPE_THETA = 10000.0
LRU_W = 512
LRU_BLOCKS = 8
LRU_BD = LRU_W // LRU_BLOCKS
CONV_W = 4
LRU_C = 8.0
GLA_HEADS = 4
GLA_DK = 64
GLA_DV = 128
GLA_KW = GLA_HEADS * GLA_DK
GLA_VW = GLA_HEADS * GLA_DV
GLA_RANK = 16
GLA_NORMALIZER = 16.0
GLA_CHUNK = 64
D_MIX = DIFF_W + LRU_W + GLA_VW
IN_SIZES = (DIFF_HEADS * 2 * DIFF_HD, DIFF_HEADS * 2 * DIFF_HD, DIFF_W,
            LRU_W, LRU_W,
            GLA_KW, GLA_KW, GLA_VW, GLA_VW, GLA_RANK)
N_IN = sum(IN_SIZES)
D_FF = -(-8 * D_MODEL // (3 * 256)) * 256
NORM_EPS = 1e-6

kernel_name = 'hybrid_diffattn_rglru_gla_parallel_heads'


def rms_norm(x, w, eps=NORM_EPS):
    x32 = x.astype(jnp.float32)
    y = x32 * lax.rsqrt(jnp.mean(x32 * x32, axis=-1, keepdims=True) + eps)
    return (y * w.astype(jnp.float32)).astype(x.dtype)


def rope(x, pos):
    d = x.shape[-1]
    inv = ROPE_THETA ** (-jnp.arange(0, d, 2, dtype=jnp.float32) / d)
    ang = pos.astype(jnp.float32)[:, None] * inv[None, :]
    bshape = (pos.shape[0],) + (1,) * (x.ndim - 3) + (d,)
    cos = jnp.concatenate([jnp.cos(ang), jnp.cos(ang)], -1).reshape(bshape)
    sin = jnp.concatenate([jnp.sin(ang), jnp.sin(ang)], -1).reshape(bshape)
    x32 = x.astype(jnp.float32)
    x1, x2 = jnp.split(x32, 2, axis=-1)
    rot = jnp.concatenate([-x2, x1], -1)
    return (x32 * cos + rot * sin).astype(x.dtype)


def diff_attention(q, k, v, lam, subln_w, lam_init):
    B, S, H, _, d = q.shape
    nb = S // Q_BLOCK
    qh = jnp.transpose(q, (0, 2, 3, 1, 4)).astype(jnp.float32) * (d ** -0.5)
    kh = jnp.transpose(k, (0, 2, 3, 1, 4)).astype(jnp.float32)
    vh = jnp.transpose(v, (0, 2, 1, 3)).astype(jnp.float32)
    qb = qh.reshape(B, H, 2, nb, Q_BLOCK, d).transpose(3, 0, 1, 2, 4, 5)
    kpos = jnp.arange(S)

    def block(args):
        q_blk, i = args
        s = jnp.einsum('bhmqd,bhmkd->bhmqk', q_blk, kh)
        qpos = i * Q_BLOCK + jnp.arange(Q_BLOCK)
        s = jnp.where(kpos[None, :] <= qpos[:, None], s, -jnp.inf)
        p = jax.nn.softmax(s, axis=-1)
        a = p[:, :, 0] - lam * p[:, :, 1]
        return jnp.einsum('bhqk,bhkv->bhqv', a, vh)

    o = lax.map(block, (qb, jnp.arange(nb)))
    o = o.transpose(1, 0, 3, 2, 4).reshape(B, S, H, 2 * d)
    o = rms_norm(o, subln_w) * (1.0 - lam_init)
    return o.reshape(B, S, H * 2 * d).astype(v.dtype)


def rglru_branch(xg, xr, conv_w, conv_b, w_a, b_a, w_x, b_x, lru_lambda):
    B, S, C = xr.shape
    gate = jax.nn.gelu(xg)
    xc = lax.conv_general_dilated(xr, conv_w[:, None, :], window_strides=(1,),
                                  padding=((CONV_W - 1, 0),),
                                  dimension_numbers=('NWC', 'WIO', 'NWC'),
                                  feature_group_count=C) + conv_b
    xb = xc.reshape(B, S, LRU_BLOCKS, LRU_BD)
    r = jax.nn.sigmoid(jnp.einsum('bsnc,ncd->bsnd', xb, w_a).reshape(B, S, C) + b_a)
    i = jax.nn.sigmoid(jnp.einsum('bsnc,ncd->bsnd', xb, w_x).reshape(B, S, C) + b_x)
    log_a = -LRU_C * r.astype(jnp.float32) * jax.nn.softplus(-lru_lambda.astype(jnp.float32))
    a = jnp.exp(log_a)
    u = jnp.sqrt(-jnp.expm1(2.0 * log_a)) * (i * xc).astype(jnp.float32)

    def combine(left, right):
        a1, b1 = left
        a2, b2 = right
        return a1 * a2, a2 * b1 + b2

    _, h = lax.associative_scan(combine, (a, u), axis=1)
    return h.astype(xr.dtype) * gate


def gla_branch(q, k, v, g_out, lr, w_gup, b_g, norm_w):
    B, S, _ = q.shape
    H, dk, dv, C = GLA_HEADS, GLA_DK, GLA_DV, GLA_CHUNK
    N = S // C
    gk = jax.nn.log_sigmoid((lr @ w_gup + b_g).astype(jnp.float32)) / GLA_NORMALIZER

    def heads(t, d):
        return t.astype(jnp.float32).reshape(B, N, C, H, d).transpose(0, 3, 1, 2, 4)

    qh = heads(q, dk) * (dk ** -0.5)
    kh = heads(k, dk)
    vh = heads(v, dv)
    bcum = jnp.cumsum(heads(gk, dk), axis=3)
    q_e = qh * jnp.exp(bcum)
    k_e = kh * jnp.exp(-bcum)
    causal = jnp.tril(jnp.ones((C, C), dtype=bool))
    att = jnp.where(causal, jnp.einsum('bhncd,bhnjd->bhncj', q_e, k_e), 0.0)
    o_intra = jnp.einsum('bhncj,bhnjv->bhncv', att, vh)
    b_last = bcum[:, :, :, -1:, :]
    kv = jnp.einsum('bhncd,bhncv->bhndv', kh * jnp.exp(b_last - bcum), vh)
    decay = jnp.exp(b_last[:, :, :, 0, :])

    def step(state, inp):
        dec, kv_n = inp
        return dec[..., None] * state + kv_n, state

    _, s_prev = lax.scan(step, jnp.zeros((B, H, dk, dv), jnp.float32),
                         (decay.transpose(2, 0, 1, 3), kv.transpose(2, 0, 1, 3, 4)))
    o_inter = jnp.einsum('bhncd,nbhdv->bhncv', q_e, s_prev)
    o = (o_intra + o_inter).transpose(0, 2, 3, 1, 4).reshape(B, S, H, dv)
    o = rms_norm(o, norm_w).reshape(B, S, H * dv)
    return (o * jax.nn.silu(g_out.astype(jnp.float32))).astype(q.dtype)


def setup_inputs(seed: int = 0) -> dict:
    key = jax.random.key(seed)
    ks = jax.random.split(key, 32)
    f32 = jnp.float32
    L = DEPTH

    def nrm(k, shape, scale):
        return jax.random.normal(k, shape, f32) * scale

    def gain(k, shape):
        return 1.0 + 0.01 * jax.random.normal(k, shape, f32)

    a_pow_c = jax.random.uniform(ks[18], (L, LRU_W), f32, 0.9, 0.999)
    log_a = jnp.log(a_pow_c) / LRU_C
    lru_lambda = log_a - jnp.log(-jnp.expm1(log_a))
    return {
        'x': jax.random.normal(ks[0], (BATCH, SEQ, D_MODEL), f32),
        'pre_mix_norm': gain(ks[1], (L, D_MODEL)),
        'post_mix_norm': gain(ks[2], (L, D_MODEL)),
        'pre_ffn_norm': gain(ks[3], (L, D_MODEL)),
        'post_ffn_norm': gain(ks[4], (L, D_MODEL)),
        'w_in': nrm(ks[5], (L, D_MODEL, N_IN), D_MODEL ** -0.5),
        'w_out': nrm(ks[6], (L, D_MIX, D_MODEL), D_MIX ** -0.5),
        'lambda_q1': nrm(ks[7], (L, DIFF_HD), 0.1),
        'lambda_k1': nrm(ks[8], (L, DIFF_HD), 0.1),
        'lambda_q2': nrm(ks[9], (L, DIFF_HD), 0.1),
        'lambda_k2': nrm(ks[10], (L, DIFF_HD), 0.1),
        'diff_subln': gain(ks[11], (L, DIFF_VD)),
        'conv_w': nrm(ks[12], (L, CONV_W, LRU_W), CONV_W ** -0.5),
        'conv_b': nrm(ks[13], (L, LRU_W), 0.01),
        'w_rgate': nrm(ks[14], (L, LRU_BLOCKS, LRU_BD, LRU_BD), LRU_BD ** -0.5),
        'b_rgate': nrm(ks[15], (L, LRU_W), 0.01),
        'w_igate': nrm(ks[16], (L, LRU_BLOCKS, LRU_BD, LRU_BD), LRU_BD ** -0.5),
        'b_igate': nrm(ks[17], (L, LRU_W), 0.01),
        'lru_lambda': lru_lambda,
        'w_gla_gate_up': nrm(ks[19], (L, GLA_RANK, GLA_KW), GLA_RANK ** -0.5),
        'b_gla_gate': nrm(ks[20], (L, GLA_KW), 0.01),
        'gla_norm': gain(ks[21], (L, GLA_DV)),
        'w_ffn_gate': nrm(ks[22], (L, D_MODEL, D_FF), D_MODEL ** -0.5),
        'w_ffn_up': nrm(ks[23], (L, D_MODEL, D_FF), D_MODEL ** -0.5),
        'w_ffn_down': nrm(ks[24], (L, D_FF, D_MODEL), D_FF ** -0.5),
    }


def reference(x, pre_mix_norm, post_mix_norm, pre_ffn_norm, post_ffn_norm, w_in, w_out,
              lambda_q1, lambda_k1, lambda_q2, lambda_k2, diff_subln,
              conv_w, conv_b, w_rgate, b_rgate, w_igate, b_igate, lru_lambda,
              w_gla_gate_up, b_gla_gate, gla_norm,
              w_ffn_gate, w_ffn_up, w_ffn_down):
    B, S, _ = x.shape
    pos = jnp.arange(S, dtype=jnp.int32)
    split_points = np.cumsum(IN_SIZES)[:-1].tolist()
    f32 = jnp.float32
    for l in range(DEPTH):
        lam_init = 0.8 - 0.6 * math.exp(-0.3 * l)
        h = rms_norm(x, pre_mix_norm[l])
        proj = jnp.einsum('bsd,dn->bsn', h, w_in[l])
        (q_d, k_d, v_d, lru_g, lru_x, g_q, g_k, g_v, g_o, g_lr) = jnp.split(proj, split_points, axis=-1)
        q_d = rope(q_d.reshape(B, S, DIFF_HEADS, 2, DIFF_HD), pos)
        k_d = rope(k_d.reshape(B, S, DIFF_HEADS, 2, DIFF_HD), pos)
        v_d = v_d.reshape(B, S, DIFF_HEADS, DIFF_VD)
        lam = (jnp.exp(jnp.sum(lambda_q1[l].astype(f32) * lambda_k1[l].astype(f32)))
               - jnp.exp(jnp.sum(lambda_q2[l].astype(f32) * lambda_k2[l].astype(f32)))
               + lam_init)
        y_attn = diff_attention(q_d, k_d, v_d, lam, diff_subln[l], lam_init).astype(x.dtype)
        y_lru = rglru_branch(lru_g, lru_x, conv_w[l], conv_b[l], w_rgate[l], b_rgate[l],
                             w_igate[l], b_igate[l], lru_lambda[l]).astype(x.dtype)
        y_gla = gla_branch(g_q, g_k, g_v, g_o, g_lr, w_gla_gate_up[l], b_gla_gate[l],
                           gla_norm[l]).astype(x.dtype)
        y_cat = jnp.concatenate([y_attn, y_lru, y_gla], axis=-1)
        mix = jnp.einsum('bsm,md->bsd', y_cat, w_out[l])
        x = x + rms_norm(mix, post_mix_norm[l])
        h = rms_norm(x, pre_ffn_norm[l])
        hid = jax.nn.silu(jnp.einsum('bsd,df->bsf', h, w_ffn_gate[l])) * jnp.einsum('bsd,df->bsf', h, w_ffn_up[l])
        ff = jnp.einsum('bsf,fd->bsd', hid, w_ffn_down[l])
        x = x + rms_norm(ff, post_ffn_norm[l])
    return x
```

```python
import functools
import math

import jax
import jax.numpy as jnp
from jax import lax
from jax.experimental import pallas as pl
from jax.experimental.pallas import tpu as pltpu

F32 = jnp.float32
BF16 = jnp.bfloat16

D_MODEL = 2048
DEPTH = 4
DIFF_HEADS = 8
DIFF_HD = 64
DIFF_VD = 2 * DIFF_HD
DIFF_W = DIFF_HEADS * DIFF_VD
ROPE_THETA = 10000.0
LRU_W = 512
LRU_BLOCKS = 8
CONV_W = 4
LRU_C = 8.0
GLA_HEADS = 4
GLA_DK = 64
GLA_DV = 128
GLA_KW = GLA_HEADS * GLA_DK
GLA_VW = GLA_HEADS * GLA_DV
GLA_RANK = 16
GLA_NORMALIZER = 16.0
GLA_CHUNK = 64
D_MIX = DIFF_W + LRU_W + GLA_VW
N_MAIN = 2 * DIFF_W + DIFF_W + 2 * LRU_W + 2 * GLA_KW + 2 * GLA_VW
D_FF = 5632
NORM_EPS = 1e-6

LANES = 128
VMEM_LIMIT = 56 * 1024 * 1024

OFF_Q, OFF_K, OFF_V = 0, DIFF_W, 2 * DIFF_W
OFF_LRU_G, OFF_LRU_X = 3 * DIFF_W, 3 * DIFF_W + LRU_W
OFF_GQ = OFF_LRU_X + LRU_W
OFF_GK = OFF_GQ + GLA_KW
OFF_GV = OFF_GK + GLA_KW
OFF_GO = OFF_GV + GLA_VW

TM_PROJ = 512
TN_PROJ = 512
TQ_ATTN = 256
T_LRU = 512
T_GLA = 256
TM_OUT = 512
TM_FFN = 512
TF_FFN = 512
TAIL = 16


def _rms(x, w):
    ms = jnp.mean(x * x, axis=-1, keepdims=True)
    return x * lax.rsqrt(ms + NORM_EPS) * w


def _inproj_kernel(x_ref, nw_ref, w_ref, wlr_ref, cos_ref, sin_ref, o_ref, lr_ref, h_ref,
                   *, n_q_tiles, n_rope_tiles):
    j = pl.program_id(1)

    @pl.when(j == 0)
    def _():
        hb = _rms(x_ref[...], nw_ref[...]).astype(BF16)
        h_ref[...] = hb
        lr_ref[...] = jnp.dot(hb, wlr_ref[...], preferred_element_type=F32).astype(BF16)

    acc = jnp.dot(h_ref[...], w_ref[...], preferred_element_type=F32)
    tm, tn = acc.shape

    @pl.when(j < n_rope_tiles)
    def _():
        cos = cos_ref[...]
        sin = sin_ref[...]
        lane = lax.broadcasted_iota(jnp.int32, (tm, LANES), 1)
        lower_half = (lane & (DIFF_HD // 2)) == 0
        scale = jnp.where(j < n_q_tiles, DIFF_HD ** -0.5, 1.0).astype(F32)
        for c in range(tn // LANES):
            a = acc[:, c * LANES:(c + 1) * LANES]
            rot = jnp.where(lower_half,
                            pltpu.roll(a, LANES - DIFF_HD // 2, 1),
                            pltpu.roll(a, DIFF_HD // 2, 1))
            o_ref[:, c * LANES:(c + 1) * LANES] = ((a * cos + rot * sin) * scale).astype(BF16)

    @pl.when(j >= n_rope_tiles)
    def _():
        o_ref[...] = acc.astype(BF16)


def _inproj(x2, norm_w, w_main, w_lr, cos_t, sin_t, seq):
    m = x2.shape[0]
    tm, tn = TM_PROJ, TN_PROJ
    assert m % tm == 0 and seq % tm == 0 and N_MAIN % tn == 0 and (2 * DIFF_W) % tn == 0
    pos_tiles = seq // tm
    kern = functools.partial(_inproj_kernel, n_q_tiles=DIFF_W // tn, n_rope_tiles=2 * DIFF_W // tn)
    return pl.pallas_call(
        kern,
        grid=(m // tm, N_MAIN // tn),
        in_specs=[
            pl.BlockSpec((tm, D_MODEL), lambda i, j: (i, 0)),
            pl.BlockSpec((1, D_MODEL), lambda i, j: (0, 0)),
            pl.BlockSpec((D_MODEL, tn), lambda i, j: (0, j)),
            pl.BlockSpec((D_MODEL, LANES), lambda i, j: (0, 0)),
            pl.BlockSpec((tm, LANES), lambda i, j: (i % pos_tiles, 0)),
            pl.BlockSpec((tm, LANES), lambda i, j: (i % pos_tiles, 0)),
        ],
        out_specs=[
            pl.BlockSpec((tm, tn), lambda i, j: (i, j)),
            pl.BlockSpec((tm, LANES), lambda i, j: (i, 0)),
        ],
        out_shape=[
            jax.ShapeDtypeStruct((m, N_MAIN), BF16),
            jax.ShapeDtypeStruct((m, LANES), BF16),
        ],
        scratch_shapes=[pltpu.VMEM((tm, D_MODEL), BF16)],
        compiler_params=pltpu.CompilerParams(
            dimension_semantics=("parallel", "arbitrary"), vmem_limit_bytes=VMEM_LIMIT),
        name="inproj",
    )(x2, norm_w, w_main, w_lr, cos_t, sin_t)


def _attn_kernel(q_ref, k_ref, v_ref, lamv_ref, sw_ref, o_ref, *, tq, lam_init):
    qi = pl.program_id(2)
    q = q_ref[0].astype(F32)
    lane = lax.broadcasted_iota(jnp.int32, (tq, LANES), 1)
    qq = jnp.concatenate([jnp.where(lane < DIFF_HD, q, 0.0),
                          jnp.where(lane >= DIFF_HD, q, 0.0)], axis=0).astype(BF16)

    def step(off, carry, mask):
        m, l, acc = carry
        kb = k_ref[0, pl.ds(off, tq), :]
        vb = v_ref[0, pl.ds(off, tq), :]
        s = lax.dot_general(qq, kb, (((1,), (1,)), ((), ())), preferred_element_type=F32)
        if mask is not None:
            s = jnp.where(mask, s, -jnp.inf)
        m_new = jnp.maximum(m, jnp.max(s, axis=-1, keepdims=True))
        alpha = jnp.exp(m - m_new)
        p = jnp.exp(s - m_new)
        l = alpha * l + jnp.sum(p, axis=-1, keepdims=True)
        acc = alpha * acc + jnp.dot(p.astype(BF16), vb, preferred_element_type=F32)
        return m_new, l, acc

    init = (jnp.full((2 * tq, 1), -jnp.inf, F32), jnp.zeros((2 * tq, 1), F32),
            jnp.zeros((2 * tq, DIFF_VD), F32))
    carry = lax.fori_loop(0, qi, lambda j, c: step(pl.multiple_of(j * tq, tq), c, None), init)
    row = lax.broadcasted_iota(jnp.int32, (2 * tq, tq), 0) & (tq - 1)
    col = lax.broadcasted_iota(jnp.int32, (2 * tq, tq), 1)
    _, l, acc = step(pl.multiple_of(qi * tq, tq), carry, col <= row)

    o = acc / l
    lv = lamv_ref[...]
    lam = (jnp.exp(jnp.sum(lv[0:1] * lv[1:2], axis=-1, keepdims=True))
           - jnp.exp(jnp.sum(lv[2:3] * lv[3:4], axis=-1, keepdims=True)) + lam_init)
    d = o[:tq] - lam * o[tq:]
    o_ref[0] = (_rms(d, sw_ref[...]) * (1.0 - lam_init)).astype(BF16)


def _diff_attention(proj3, lamv, subln_w, lam_init):
    b, s, _ = proj3.shape
    tq = TQ_ATTN
    assert s % tq == 0 and tq & (tq - 1) == 0
    kb0, vb0 = OFF_K // LANES, OFF_V // LANES
    kern = functools.partial(_attn_kernel, tq=tq, lam_init=lam_init)
    return pl.pallas_call(
        kern,
        grid=(b, DIFF_HEADS, s // tq),
        in_specs=[
            pl.BlockSpec((1, tq, LANES), lambda bi, h, qi: (bi, qi, h)),
            pl.BlockSpec((1, s, LANES), lambda bi, h, qi: (bi, 0, kb0 + h)),
            pl.BlockSpec((1, s, LANES), lambda bi, h, qi: (bi, 0, vb0 + h)),
            pl.BlockSpec((4, LANES), lambda bi, h, qi: (0, 0)),
            pl.BlockSpec((1, DIFF_VD), lambda bi, h, qi: (0, 0)),
        ],
        out_specs=pl.BlockSpec((1, tq, LANES), lambda bi, h, qi: (bi, qi, h)),
        out_shape=jax.ShapeDtypeStruct((b, s, DIFF_W), BF16),
        compiler_params=pltpu.CompilerParams(
            dimension_semantics=("parallel", "parallel", "arbitrary"), vmem_limit_bytes=VMEM_LIMIT),
        name="diff_attn",
    )(proj3, proj3, proj3, lamv, subln_w)


def _shift_rows(x, d, fill):
    row = lax.broadcasted_iota(jnp.int32, x.shape, 0)
    return jnp.where(row >= d, pltpu.roll(x, d, 0), fill)


def _lru_kernel(g_ref, x_ref, tail_ref, cw_ref, cb_ref, wg_ref, bg_ref, lam_ref, o_ref, h_ref, *, t):
    ti = pl.program_id(1)

    @pl.when(ti == 0)
    def _():
        h_ref[...] = jnp.zeros_like(h_ref)

    xr = x_ref[0].astype(F32)
    tail = jnp.where(ti > 0, tail_ref[0].astype(F32), 0.0)
    ext = jnp.concatenate([tail, xr], axis=0)
    cw = cw_ref[...]
    xc = cb_ref[...] + cw[CONV_W - 1:CONV_W] * xr
    for d in range(1, CONV_W):
        xc = xc + cw[CONV_W - 1 - d:CONV_W - d] * pltpu.roll(ext, d, 0)[TAIL:]

    gates = jnp.dot(xc.astype(BF16), wg_ref[...], preferred_element_type=F32) + bg_ref[...]
    r = jax.nn.sigmoid(gates[:, :LRU_W])
    i = jax.nn.sigmoid(gates[:, LRU_W:])
    lam = lam_ref[...]
    softplus_neg_lam = jnp.maximum(-lam, 0.0) + jnp.log1p(jnp.exp(-jnp.abs(lam)))
    log_a = -LRU_C * r * softplus_neg_lam
    a = jnp.exp(log_a)
    u = jnp.sqrt(-jnp.tanh(log_a) * (a * a + 1.0)) * (i * xc)

    d = 1
    while d < t:
        u = a * _shift_rows(u, d, 0.0) + u
        a = a * _shift_rows(a, d, 1.0)
        d *= 2
    h = u + a * h_ref[...]
    h_ref[...] = h[t - 1:t]

    xg = g_ref[0].astype(F32)
    gelu = 0.5 * xg * (1.0 + jnp.tanh(math.sqrt(2.0 / math.pi) * (xg + 0.044715 * (xg * xg * xg))))
    o_ref[0] = (h * gelu).astype(BF16)


def _rglru(proj3, conv_w, conv_b, w_gates, b_gates, lru_lambda):
    b, s, _ = proj3.shape
    t = T_LRU
    assert s % t == 0 and t % TAIL == 0
    gb, xb = OFF_LRU_G // LRU_W, OFF_LRU_X // LRU_W
    kern = functools.partial(_lru_kernel, t=t)
    return pl.pallas_call(
        kern,
        grid=(b, s // t),
        in_specs=[
            pl.BlockSpec((1, t, LRU_W), lambda bi, ti: (bi, ti, gb)),
            pl.BlockSpec((1, t, LRU_W), lambda bi, ti: (bi, ti, xb)),
            pl.BlockSpec((1, TAIL, LRU_W), lambda bi, ti: (bi, jnp.maximum(ti * (t // TAIL) - 1, 0), xb)),
            pl.BlockSpec((CONV_W, LRU_W), lambda bi, ti: (0, 0)),
            pl.BlockSpec((1, LRU_W), lambda bi, ti: (0, 0)),
            pl.BlockSpec((LRU_W, 2 * LRU_W), lambda bi, ti: (0, 0)),
            pl.BlockSpec((1, 2 * LRU_W), lambda bi, ti: (0, 0)),
            pl.BlockSpec((1, LRU_W), lambda bi, ti: (0, 0)),
        ],
        out_specs=pl.BlockSpec((1, t, LRU_W), lambda bi, ti: (bi, ti, 0)),
        out_shape=jax.ShapeDtypeStruct((b, s, LRU_W), BF16),
        scratch_shapes=[pltpu.VMEM((1, LRU_W), F32)],
        compiler_params=pltpu.CompilerParams(
            dimension_semantics=("parallel", "arbitrary"), vmem_limit_bytes=VMEM_LIMIT),
        name="rglru",
    )(proj3, proj3, proj3, conv_w, conv_b, w_gates, b_gates, lru_lambda)


def _gla_kernel(q_ref, k_ref, v_ref, go_ref, lr_ref, wg_ref, bg_ref, nw_ref, o_ref, st_ref, *, t):
    ti = pl.program_id(1)

    @pl.when(ti == 0)
    def _():
        st_ref[...] = jnp.zeros_like(st_ref)

    c = GLA_CHUNK
    n = t // c
    z = jnp.dot(lr_ref[0], wg_ref[...], preferred_element_type=F32) + bg_ref[...]
    gk = (jnp.minimum(z, 0.0) - jnp.log1p(jnp.exp(-jnp.abs(z)))) * (1.0 / GLA_NORMALIZER)

    row = lax.broadcasted_iota(jnp.int32, (t, GLA_KW), 0)
    rc = row & (c - 1)
    bc = gk
    d = 1
    while d < c:
        bc = bc + jnp.where(rc >= d, pltpu.roll(bc, d, 0), 0.0)
        d *= 2
    bl3 = jnp.broadcast_to(bc.reshape(n, c, GLA_KW)[:, c - 1:c, :], (n, c, GLA_KW))
    bl = bl3.reshape(t, GLA_KW)

    q = q_ref[0].astype(F32) * (GLA_DK ** -0.5)
    k = k_ref[0].astype(F32)
    qe = q * jnp.exp(bc)
    ke = k * jnp.exp(-bc)
    kd = k * jnp.exp(bl - bc)
    dec = jnp.exp(bl)

    lane = lax.broadcasted_iota(jnp.int32, (t, GLA_KW), 1)
    col = lax.broadcasted_iota(jnp.int32, (t, t), 1)
    rowt = lax.broadcasted_iota(jnp.int32, (t, t), 0)
    causal = (col <= rowt) & ((col & -c) == (rowt & -c))
    qe_b = qe.astype(BF16)
    nw = nw_ref[...]
    for h in range(GLA_HEADS):
        in_head = (lane >= h * GLA_DK) & (lane < (h + 1) * GLA_DK)
        ke_h = jnp.where(in_head, ke, 0.0).astype(BF16)
        kd_h = jnp.where(in_head, kd, 0.0).astype(BF16)
        v_h = v_ref[0, :, h * GLA_DV:(h + 1) * GLA_DV]
        att = lax.dot_general(qe_b, ke_h, (((1,), (1,)), ((), ())), preferred_element_type=F32)
        att = jnp.where(causal, att, 0.0)
        o_intra = jnp.dot(att.astype(BF16), v_h, preferred_element_type=F32)
        st = st_ref[h]
        o_inter = []
        for ci in range(n):
            sl = slice(ci * c, (ci + 1) * c)
            o_inter.append(lax.dot_general(qe_b[sl], st.astype(BF16), (((1,), (1,)), ((), ())),
                                           preferred_element_type=F32))
            kvt = lax.dot_general(v_h[sl], kd_h[sl], (((0,), (0,)), ((), ())),
                                  preferred_element_type=F32)
            st = st * dec[ci * c:ci * c + 1] + kvt
        st_ref[h] = st
        o = o_intra + jnp.concatenate(o_inter, axis=0)
        go = go_ref[0, :, h * GLA_DV:(h + 1) * GLA_DV].astype(F32)
        o_ref[0, :, h * GLA_DV:(h + 1) * GLA_DV] = (_rms(o, nw) * (go * jax.nn.sigmoid(go))).astype(BF16)


def _gla(proj3, lr3, w_gup, b_g, norm_w):
    b, s, _ = proj3.shape
    t = T_GLA
    assert s % t == 0 and t % GLA_CHUNK == 0
    kern = functools.partial(_gla_kernel, t=t)
    return pl.pallas_call(
        kern,
        grid=(b, s // t),
        in_specs=[
            pl.BlockSpec((1, t, GLA_KW), lambda bi, ti: (bi, ti, OFF_GQ // GLA_KW)),
            pl.BlockSpec((1, t, GLA_KW), lambda bi, ti: (bi, ti, OFF_GK // GLA_KW)),
            pl.BlockSpec((1, t, GLA_VW), lambda bi, ti: (bi, ti, OFF_GV // GLA_VW)),
            pl.BlockSpec((1, t, GLA_VW), lambda bi, ti: (bi, ti, OFF_GO // GLA_VW)),
            pl.BlockSpec((1, t, LANES), lambda bi, ti: (bi, ti, 0)),
            pl.BlockSpec((LANES, GLA_KW), lambda bi, ti: (0, 0)),
            pl.BlockSpec((1, GLA_KW), lambda bi, ti: (0, 0)),
            pl.BlockSpec((1, GLA_DV), lambda bi, ti: (0, 0)),
        ],
        out_specs=pl.BlockSpec((1, t, GLA_VW), lambda bi, ti: (bi, ti, 0)),
        out_shape=jax.ShapeDtypeStruct((b, s, GLA_VW), BF16),
        scratch_shapes=[pltpu.VMEM((GLA_HEADS, GLA_DV, GLA_KW), F32)],
        compiler_params=pltpu.CompilerParams(
            dimension_semantics=("parallel", "arbitrary"), vmem_limit_bytes=VMEM_LIMIT),
        name="gla",
    )(proj3, proj3, proj3, proj3, lr3, w_gup, b_g, norm_w)


def _outproj_kernel(ya_ref, yl_ref, yg_ref, w_ref, x_ref, nw_ref, o_ref):
    y = jnp.concatenate([ya_ref[...], yl_ref[...], yg_ref[...]], axis=1)
    mix = jnp.dot(y, w_ref[...], preferred_element_type=F32)
    o_ref[...] = x_ref[...] + _rms(mix, nw_ref[...])


def _outproj(ya, yl, yg, w_out, x2, norm_w):
    m = x2.shape[0]
    tm = TM_OUT
    assert m % tm == 0
    return pl.pallas_call(
        _outproj_kernel,
        grid=(m // tm,),
        in_specs=[
            pl.BlockSpec((tm, DIFF_W), lambda i: (i, 0)),
            pl.BlockSpec((tm, LRU_W), lambda i: (i, 0)),
            pl.BlockSpec((tm, GLA_VW), lambda i: (i, 0)),
            pl.BlockSpec((D_MIX, D_MODEL), lambda i: (0, 0)),
            pl.BlockSpec((tm, D_MODEL), lambda i: (i, 0)),
            pl.BlockSpec((1, D_MODEL), lambda i: (0, 0)),
        ],
        out_specs=pl.BlockSpec((tm, D_MODEL), lambda i: (i, 0)),
        out_shape=jax.ShapeDtypeStruct((m, D_MODEL), F32),
        compiler_params=pltpu.CompilerParams(
            dimension_semantics=("parallel",), vmem_limit_bytes=VMEM_LIMIT),
        name="outproj",
    )(ya, yl, yg, w_out, x2, norm_w)


def _ffn_kernel(x_ref, nw1_ref, wg_ref, wu_ref, wd_ref, nw2_ref, o_ref, h_ref, acc_ref):
    f = pl.program_id(1)

    @pl.when(f == 0)
    def _():
        h_ref[...] = _rms(x_ref[...], nw1_ref[...]).astype(BF16)

    h = h_ref[...]
    g = jnp.dot(h, wg_ref[...], preferred_element_type=F32)
    u = jnp.dot(h, wu_ref[...], preferred_element_type=F32)
    hid = (g * jax.nn.sigmoid(g) * u).astype(BF16)
    part = jnp.dot(hid, wd_ref[...], preferred_element_type=F32)

    @pl.when(f == 0)
    def _():
        acc_ref[...] = part

    @pl.when(f > 0)
    def _():
        acc_ref[...] += part

    @pl.when(f == pl.num_programs(1) - 1)
    def _():
        o_ref[...] = x_ref[...] + _rms(acc_ref[...], nw2_ref[...])


def _ffn(x2, nw1, w_gate, w_up, w_down, nw2):
    m = x2.shape[0]
    tm, tf = TM_FFN, TF_FFN
    assert m % tm == 0 and D_FF % tf == 0
    return pl.pallas_call(
        _ffn_kernel,
        grid=(m // tm, D_FF // tf),
        in_specs=[
            pl.BlockSpec((tm, D_MODEL), lambda i, f: (i, 0)),
            pl.BlockSpec((1, D_MODEL), lambda i, f: (0, 0)),
            pl.BlockSpec((D_MODEL, tf), lambda i, f: (0, f)),
            pl.BlockSpec((D_MODEL, tf), lambda i, f: (0, f)),
            pl.BlockSpec((tf, D_MODEL), lambda i, f: (f, 0)),
            pl.BlockSpec((1, D_MODEL), lambda i, f: (0, 0)),
        ],
        out_specs=pl.BlockSpec((tm, D_MODEL), lambda i, f: (i, 0)),
        out_shape=jax.ShapeDtypeStruct((m, D_MODEL), F32),
        scratch_shapes=[pltpu.VMEM((tm, D_MODEL), BF16), pltpu.VMEM((tm, D_MODEL), F32)],
        compiler_params=pltpu.CompilerParams(
            dimension_semantics=("parallel", "arbitrary"), vmem_limit_bytes=VMEM_LIMIT),
        name="ffn",
    )(x2, nw1, w_gate, w_up, w_down, nw2)


def _rope_tables(seq):
    half = DIFF_HD // 2
    inv = ROPE_THETA ** (-jnp.arange(0, DIFF_HD, 2, dtype=F32) / DIFF_HD)
    ang = jnp.arange(seq, dtype=F32)[:, None] * inv[None, :]
    reps = LANES // half
    cos = jnp.tile(jnp.cos(ang), (1, reps))
    sign = jnp.tile(jnp.concatenate([-jnp.ones((half,), F32), jnp.ones((half,), F32)]), LANES // DIFF_HD)
    sin = jnp.tile(jnp.sin(ang), (1, reps)) * sign[None, :]
    return cos, sin


def _block_diag(w):
    n, c, d = w.shape
    eye = jnp.eye(n, dtype=w.dtype)
    return (eye[:, None, :, None] * w[:, :, None, :]).reshape(n * c, n * d)


@jax.jit
def _forward(x, pre_mix_norm, post_mix_norm, pre_ffn_norm, post_ffn_norm, w_in, w_out,
             lambda_q1, lambda_k1, lambda_q2, lambda_k2, diff_subln,
             conv_w, conv_b, w_rgate, b_rgate, w_igate, b_igate, lru_lambda,
             w_gla_gate_up, b_gla_gate, gla_norm, w_ffn_gate, w_ffn_up, w_ffn_down):
    b, s, dm = x.shape
    m = b * s
    cos_t, sin_t = _rope_tables(s)
    x2 = x.reshape(m, dm)
    row = lambda v: v.reshape(1, -1).astype(F32)
    for l in range(DEPTH):
        lam_init = 0.8 - 0.6 * math.exp(-0.3 * l)
        w_main = w_in[l, :, :N_MAIN].astype(BF16)
        w_lr = jnp.pad(w_in[l, :, N_MAIN:], ((0, 0), (0, LANES - GLA_RANK))).astype(BF16)
        proj, lr = _inproj(x2, row(pre_mix_norm[l]), w_main, w_lr, cos_t, sin_t, s)
        proj3 = proj.reshape(b, s, N_MAIN)
        lr3 = lr.reshape(b, s, LANES)

        lamv = jnp.pad(jnp.stack([lambda_q1[l], lambda_k1[l], lambda_q2[l], lambda_k2[l]]).astype(F32),
                       ((0, 0), (0, LANES - DIFF_HD)))
        y_attn = _diff_attention(proj3, lamv, row(diff_subln[l]), lam_init)

        w_gates = jnp.concatenate([_block_diag(w_rgate[l]), _block_diag(w_igate[l])], axis=1).astype(BF16)
        b_gates = jnp.concatenate([b_rgate[l], b_igate[l]]).reshape(1, -1).astype(F32)
        y_lru = _rglru(proj3, conv_w[l].astype(F32), row(conv_b[l]), w_gates, b_gates, row(lru_lambda[l]))

        w_gup = jnp.pad(w_gla_gate_up[l], ((0, LANES - GLA_RANK), (0, 0))).astype(BF16)
        y_gla = _gla(proj3, lr3, w_gup, row(b_gla_gate[l]), row(gla_norm[l]))

        x2 = _outproj(y_attn.reshape(m, DIFF_W), y_lru.reshape(m, LRU_W), y_gla.reshape(m, GLA_VW),
                      w_out[l].astype(BF16), x2, row(post_mix_norm[l]))
        x2 = _ffn(x2, row(pre_ffn_norm[l]), w_ffn_gate[l].astype(BF16), w_ffn_up[l].astype(BF16),
                  w_ffn_down[l].astype(BF16), row(post_ffn_norm[l]))
    return x2.reshape(b, s, dm)


def kernel(x, pre_mix_norm, post_mix_norm, pre_ffn_norm, post_ffn_norm, w_in, w_out, lambda_q1, lambda_k1, lambda_q2, lambda_k2, diff_subln, conv_w, conv_b, w_rgate, b_rgate, w_igate, b_igate, lru_lambda, w_gla_gate_up, b_gla_gate, gla_norm, w_ffn_gate, w_ffn_up, w_ffn_down):
    return _forward(x, pre_mix_norm, post_mix_norm, pre_ffn_norm, post_ffn_norm, w_in, w_out,
                    lambda_q1, lambda_k1, lambda_q2, lambda_k2, diff_subln,
                    conv_w, conv_b, w_rgate, b_rgate, w_igate, b_igate, lru_lambda,
                    w_gla_gate_up, b_gla_gate, gla_norm, w_ffn_gate, w_ffn_up, w_ffn_down)
```

```python
import functools
import math

import jax
import jax.numpy as jnp
from jax import lax
from jax.experimental import pallas as pl
from jax.experimental.pallas import tpu as pltpu

F32 = jnp.float32
BF16 = jnp.bfloat16

D_MODEL = 2048
DEPTH = 4
DIFF_HEADS = 8
DIFF_HD = 64
DIFF_VD = 2 * DIFF_HD
DIFF_W = DIFF_HEADS * DIFF_VD
ROPE_THETA = 10000.0
LRU_W = 512
LRU_BLOCKS = 8
CONV_W = 4
LRU_C = 8.0
GLA_HEADS = 4
GLA_DK = 64
GLA_DV = 128
GLA_KW = GLA_HEADS * GLA_DK
GLA_VW = GLA_HEADS * GLA_DV
GLA_RANK = 16
GLA_NORMALIZER = 16.0
GLA_CHUNK = 64
D_MIX = DIFF_W + LRU_W + GLA_VW
N_MAIN = 2 * DIFF_W + DIFF_W + 2 * LRU_W + 2 * GLA_KW + 2 * GLA_VW
D_FF = 5632
NORM_EPS = 1e-6

LANES = 128
VMEM_LIMIT = 56 * 1024 * 1024

OFF_Q, OFF_K, OFF_V = 0, DIFF_W, 2 * DIFF_W
OFF_LRU_G, OFF_LRU_X = 3 * DIFF_W, 3 * DIFF_W + LRU_W
OFF_GQ = OFF_LRU_X + LRU_W
OFF_GK = OFF_GQ + GLA_KW
OFF_GV = OFF_GK + GLA_KW
OFF_GO = OFF_GV + GLA_VW

TM_PROJ = 1024
TN_PROJ = 512
TQ_ATTN = 512
Q_SCALE = DIFF_HD ** -0.5 * math.log2(math.e)
T_LRU = 512
T_GLA = 256
TM_OUT = 512
TM_FFN = 512
TF_FFN = 512
TAIL = 16


def _rms(x, w):
    ms = jnp.mean(x * x, axis=-1, keepdims=True)
    return x * lax.rsqrt(ms + NORM_EPS) * w


def _inproj_kernel(x_ref, nw_ref, w_ref, wlr_ref, o_ref, lr_ref, h_ref):
    @pl.when(pl.program_id(1) == 0)
    def _():
        hb = _rms(x_ref[...], nw_ref[...]).astype(BF16)
        h_ref[...] = hb
        lr_ref[...] = jnp.dot(hb, wlr_ref[...], preferred_element_type=F32).astype(BF16)

    h = h_ref[...]
    half = o_ref.shape[1] // 2
    for c in range(2):
        cols = slice(c * half, (c + 1) * half)
        o_ref[:, cols] = jnp.dot(h, w_ref[:, cols], preferred_element_type=F32).astype(BF16)


def _inproj(x2, norm_w, w_main, w_lr):
    m = x2.shape[0]
    tm, tn = TM_PROJ, TN_PROJ
    assert m % tm == 0 and N_MAIN % tn == 0
    return pl.pallas_call(
        _inproj_kernel,
        grid=(m // tm, N_MAIN // tn),
        in_specs=[
            pl.BlockSpec((tm, D_MODEL), lambda i, j: (i, 0)),
            pl.BlockSpec((1, D_MODEL), lambda i, j: (0, 0)),
            pl.BlockSpec((D_MODEL, tn), lambda i, j: (0, j)),
            pl.BlockSpec((D_MODEL, LANES), lambda i, j: (0, 0)),
        ],
        out_specs=[
            pl.BlockSpec((tm, tn), lambda i, j: (i, j)),
            pl.BlockSpec((tm, LANES), lambda i, j: (i, 0)),
        ],
        out_shape=[
            jax.ShapeDtypeStruct((m, N_MAIN), BF16),
            jax.ShapeDtypeStruct((m, LANES), BF16),
        ],
        scratch_shapes=[pltpu.VMEM((tm, D_MODEL), BF16)],
        compiler_params=pltpu.CompilerParams(
            dimension_semantics=("parallel", "arbitrary"), vmem_limit_bytes=VMEM_LIMIT),
        name="inproj",
    )(x2, norm_w, w_main, w_lr)


def _rope(a, cos, sin):
    lane = lax.broadcasted_iota(jnp.int32, a.shape, 1)
    lower_half = (lane & (DIFF_HD // 2)) == 0
    rot = jnp.where(lower_half,
                    pltpu.roll(a, LANES - DIFF_HD // 2, 1),
                    pltpu.roll(a, DIFF_HD // 2, 1))
    return a * cos + rot * sin


def _stack_maps(q):
    lane = lax.broadcasted_iota(jnp.int32, q.shape, 1)
    return jnp.concatenate([jnp.where(lane < DIFF_HD, q, 0.0),
                            jnp.where(lane >= DIFF_HD, q, 0.0)], axis=0).astype(BF16)


def _attn_kernel(q_ref, k_ref, v_ref, tab_ref, lamv_ref, sw_ref, o_ref,
                 kr_ref, sa_ref, sb_ref, qq_ref, p_ref, m_ref, l_ref, acc_ref, *, tq, tk, lam_init):
    qi = pl.program_id(2)
    nq = pl.num_programs(2)

    def rotated_q(tile):
        rows = pl.ds(pl.multiple_of(tile * tq, tq), tq)
        q = _rope(q_ref[0, rows, :].astype(F32), tab_ref[0, rows, :], tab_ref[1, rows, :])
        return _stack_maps(q * Q_SCALE)

    def scores(qq, blk, s_ref):
        kb = kr_ref[pl.ds(pl.multiple_of(blk * tk, tk), tk), :]
        s_ref[...] = lax.dot_general(qq, kb, (((1,), (1,)), ((), ())), preferred_element_type=F32)

    def process(s_ref, blk, mask):
        s = s_ref[...]
        if mask is not None:
            s = jnp.where(mask, s, -jnp.inf)
        m = m_ref[...]
        m_new = jnp.maximum(m, jnp.max(s, axis=-1, keepdims=True))
        alpha = jnp.exp2(m - m_new)
        p = [jnp.exp2(s[:, c * LANES:(c + 1) * LANES] - m_new) for c in range(tk // LANES)]
        m_ref[...] = m_new
        l_ref[...] = alpha * l_ref[...] + functools.reduce(lambda a, b: a + b, p)
        p_ref[...] = jnp.concatenate(p, axis=1).astype(BF16)
        acc_ref[...] = alpha * acc_ref[...]
        vb = v_ref[0, pl.ds(pl.multiple_of(blk * tk, tk), tk), :]
        acc_ref[...] += jnp.dot(p_ref[...], vb, preferred_element_type=F32)

    qq_ref[...] = rotated_q(qi)
    m_ref[...] = jnp.full_like(m_ref, -jnp.inf)
    l_ref[...] = jnp.zeros_like(l_ref)
    acc_ref[...] = jnp.zeros_like(acc_ref)

    @pl.when(qi == 0)
    def _():
        kr_ref[...] = _rope(k_ref[0].astype(F32), tab_ref[0], tab_ref[1]).astype(BF16)
        scores(qq_ref[...], 0, sa_ref)

    def pair(p, carry):
        qq = qq_ref[...]
        scores(qq, 2 * p + 1, sb_ref)
        process(sa_ref, 2 * p, None)
        scores(qq, 2 * p + 2, sa_ref)
        process(sb_ref, 2 * p + 1, None)
        return carry

    lax.fori_loop(0, qi, pair, 0)

    row = lax.broadcasted_iota(jnp.int32, (2 * tq, tk), 0) & (tq - 1)
    col = lax.broadcasted_iota(jnp.int32, (2 * tq, tk), 1)
    d0 = 2 * qi
    scores(qq_ref[...], d0 + 1, sb_ref)
    process(sa_ref, d0, col <= row)
    scores(rotated_q(jnp.minimum(qi + 1, nq - 1)), 0, sa_ref)
    process(sb_ref, d0 + 1, col + tk <= row)

    o = acc_ref[...] / jnp.sum(l_ref[...], axis=-1, keepdims=True)
    lv = lamv_ref[...]
    lam = (jnp.exp(jnp.sum(lv[0:1] * lv[1:2], axis=-1, keepdims=True))
           - jnp.exp(jnp.sum(lv[2:3] * lv[3:4], axis=-1, keepdims=True)) + lam_init)
    d = o[:tq] - lam * o[tq:]
    o_ref[0] = (_rms(d, sw_ref[...]) * (1.0 - lam_init)).astype(BF16)


def _diff_attention(proj3, rope_tab, lamv, subln_w, lam_init):
    b, s, _ = proj3.shape
    tq = TQ_ATTN
    tk = tq // 2
    assert s % tq == 0 and tq & (tq - 1) == 0
    kb0, vb0 = OFF_K // LANES, OFF_V // LANES
    kern = functools.partial(_attn_kernel, tq=tq, tk=tk, lam_init=lam_init)
    return pl.pallas_call(
        kern,
        grid=(b, DIFF_HEADS, s // tq),
        in_specs=[
            pl.BlockSpec((1, s, LANES), lambda bi, h, qi: (bi, 0, h)),
            pl.BlockSpec((1, s, LANES), lambda bi, h, qi: (bi, 0, kb0 + h)),
            pl.BlockSpec((1, s, LANES), lambda bi, h, qi: (bi, 0, vb0 + h)),
            pl.BlockSpec((2, s, LANES), lambda bi, h, qi: (0, 0, 0)),
            pl.BlockSpec((4, LANES), lambda bi, h, qi: (0, 0)),
            pl.BlockSpec((1, DIFF_VD), lambda bi, h, qi: (0, 0)),
        ],
        out_specs=pl.BlockSpec((1, tq, LANES), lambda bi, h, qi: (bi, qi, h)),
        out_shape=jax.ShapeDtypeStruct((b, s, DIFF_W), BF16),
        scratch_shapes=[
            pltpu.VMEM((s, LANES), BF16),
            pltpu.VMEM((2 * tq, tk), F32), pltpu.VMEM((2 * tq, tk), F32),
            pltpu.VMEM((2 * tq, LANES), BF16), pltpu.VMEM((2 * tq, tk), BF16),
            pltpu.VMEM((2 * tq, LANES), F32), pltpu.VMEM((2 * tq, LANES), F32),
            pltpu.VMEM((2 * tq, DIFF_VD), F32),
        ],
        compiler_params=pltpu.CompilerParams(
            dimension_semantics=("parallel", "parallel", "arbitrary"), vmem_limit_bytes=VMEM_LIMIT),
        name="diff_attn",
    )(proj3, proj3, proj3, rope_tab, lamv, subln_w)


def _shift_rows(x, d, fill):
    row = lax.broadcasted_iota(jnp.int32, x.shape, 0)
    return jnp.where(row >= d, pltpu.roll(x, d, 0), fill)


def _lru_kernel(g_ref, x_ref, tail_ref, cw_ref, cb_ref, wg_ref, bg_ref, lam_ref, o_ref, h_ref, *, t):
    ti = pl.program_id(1)

    @pl.when(ti == 0)
    def _():
        h_ref[...] = jnp.zeros_like(h_ref)

    xr = x_ref[0].astype(F32)
    tail = jnp.where(ti > 0, tail_ref[0].astype(F32), 0.0)
    ext = jnp.concatenate([tail, xr], axis=0)
    cw = cw_ref[...]
    xc = cb_ref[...] + cw[CONV_W - 1:CONV_W] * xr
    for d in range(1, CONV_W):
        xc = xc + cw[CONV_W - 1 - d:CONV_W - d] * pltpu.roll(ext, d, 0)[TAIL:]

    gates = jnp.dot(xc.astype(BF16), wg_ref[...], preferred_element_type=F32) + bg_ref[...]
    r = jax.nn.sigmoid(gates[:, :LRU_W])
    i = jax.nn.sigmoid(gates[:, LRU_W:])
    lam = lam_ref[...]
    softplus_neg_lam = jnp.maximum(-lam, 0.0) + jnp.log1p(jnp.exp(-jnp.abs(lam)))
    log_a = -LRU_C * r * softplus_neg_lam
    a = jnp.exp(log_a)
    u = jnp.sqrt(-jnp.tanh(log_a) * (a * a + 1.0)) * (i * xc)

    d = 1
    while d < t:
        u = a * _shift_rows(u, d, 0.0) + u
        a = a * _shift_rows(a, d, 1.0)
        d *= 2
    h = u + a * h_ref[...]
    h_ref[...] = h[t - 1:t]

    xg = g_ref[0].astype(F32)
    gelu = 0.5 * xg * (1.0 + jnp.tanh(math.sqrt(2.0 / math.pi) * (xg + 0.044715 * (xg * xg * xg))))
    o_ref[0] = (h * gelu).astype(BF16)


def _rglru(proj3, conv_w, conv_b, w_gates, b_gates, lru_lambda):
    b, s, _ = proj3.shape
    t = T_LRU
    assert s % t == 0 and t % TAIL == 0
    gb, xb = OFF_LRU_G // LRU_W, OFF_LRU_X // LRU_W
    kern = functools.partial(_lru_kernel, t=t)
    return pl.pallas_call(
        kern,
        grid=(b, s // t),
        in_specs=[
            pl.BlockSpec((1, t, LRU_W), lambda bi, ti: (bi, ti, gb)),
            pl.BlockSpec((1, t, LRU_W), lambda bi, ti: (bi, ti, xb)),
            pl.BlockSpec((1, TAIL, LRU_W), lambda bi, ti: (bi, jnp.maximum(ti * (t // TAIL) - 1, 0), xb)),
            pl.BlockSpec((CONV_W, LRU_W), lambda bi, ti: (0, 0)),
            pl.BlockSpec((1, LRU_W), lambda bi, ti: (0, 0)),
            pl.BlockSpec((LRU_W, 2 * LRU_W), lambda bi, ti: (0, 0)),
            pl.BlockSpec((1, 2 * LRU_W), lambda bi, ti: (0, 0)),
            pl.BlockSpec((1, LRU_W), lambda bi, ti: (0, 0)),
        ],
        out_specs=pl.BlockSpec((1, t, LRU_W), lambda bi, ti: (bi, ti, 0)),
        out_shape=jax.ShapeDtypeStruct((b, s, LRU_W), BF16),
        scratch_shapes=[pltpu.VMEM((1, LRU_W), F32)],
        compiler_params=pltpu.CompilerParams(
            dimension_semantics=("parallel", "arbitrary"), vmem_limit_bytes=VMEM_LIMIT),
        name="rglru",
    )(proj3, proj3, proj3, conv_w, conv_b, w_gates, b_gates, lru_lambda)


def _gla_kernel(q_ref, k_ref, v_ref, go_ref, lr_ref, wg_ref, bg_ref, nw_ref, o_ref, st_ref, *, t):
    ti = pl.program_id(1)

    @pl.when(ti == 0)
    def _():
        st_ref[...] = jnp.zeros_like(st_ref)

    c = GLA_CHUNK
    n = t // c
    z = jnp.dot(lr_ref[0], wg_ref[...], preferred_element_type=F32) + bg_ref[...]
    gk = (jnp.minimum(z, 0.0) - jnp.log1p(jnp.exp(-jnp.abs(z)))) * (1.0 / GLA_NORMALIZER)

    row = lax.broadcasted_iota(jnp.int32, (t, GLA_KW), 0)
    rc = row & (c - 1)
    bc = gk
    d = 1
    while d < c:
        bc = bc + jnp.where(rc >= d, pltpu.roll(bc, d, 0), 0.0)
        d *= 2
    bl3 = jnp.broadcast_to(bc.reshape(n, c, GLA_KW)[:, c - 1:c, :], (n, c, GLA_KW))
    bl = bl3.reshape(t, GLA_KW)

    q = q_ref[0].astype(F32) * (GLA_DK ** -0.5)
    k = k_ref[0].astype(F32)
    qe = q * jnp.exp(bc)
    ke = k * jnp.exp(-bc)
    kd = k * jnp.exp(bl - bc)
    dec = jnp.exp(bl)

    lane = lax.broadcasted_iota(jnp.int32, (t, GLA_KW), 1)
    col = lax.broadcasted_iota(jnp.int32, (t, t), 1)
    rowt = lax.broadcasted_iota(jnp.int32, (t, t), 0)
    causal = (col <= rowt) & ((col & -c) == (rowt & -c))
    qe_b = qe.astype(BF16)
    nw = nw_ref[...]
    for h in range(GLA_HEADS):
        in_head = (lane >= h * GLA_DK) & (lane < (h + 1) * GLA_DK)
        ke_h = jnp.where(in_head, ke, 0.0).astype(BF16)
        kd_h = jnp.where(in_head, kd, 0.0).astype(BF16)
        v_h = v_ref[0, :, h * GLA_DV:(h + 1) * GLA_DV]
        att = lax.dot_general(qe_b, ke_h, (((1,), (1,)), ((), ())), preferred_element_type=F32)
        att = jnp.where(causal, att, 0.0)
        o_intra = jnp.dot(att.astype(BF16), v_h, preferred_element_type=F32)
        st = st_ref[h]
        o_inter = []
        for ci in range(n):
            sl = slice(ci * c, (ci + 1) * c)
            o_inter.append(lax.dot_general(qe_b[sl], st.astype(BF16), (((1,), (1,)), ((), ())),
                                           preferred_element_type=F32))
            kvt = lax.dot_general(v_h[sl], kd_h[sl], (((0,), (0,)), ((), ())),
                                  preferred_element_type=F32)
            st = st * dec[ci * c:ci * c + 1] + kvt
        st_ref[h] = st
        o = o_intra + jnp.concatenate(o_inter, axis=0)
        go = go_ref[0, :, h * GLA_DV:(h + 1) * GLA_DV].astype(F32)
        o_ref[0, :, h * GLA_DV:(h + 1) * GLA_DV] = (_rms(o, nw) * (go * jax.nn.sigmoid(go))).astype(BF16)


def _gla(proj3, lr3, w_gup, b_g, norm_w):
    b, s, _ = proj3.shape
    t = T_GLA
    assert s % t == 0 and t % GLA_CHUNK == 0
    kern = functools.partial(_gla_kernel, t=t)
    return pl.pallas_call(
        kern,
        grid=(b, s // t),
        in_specs=[
            pl.BlockSpec((1, t, GLA_KW), lambda bi, ti: (bi, ti, OFF_GQ // GLA_KW)),
            pl.BlockSpec((1, t, GLA_KW), lambda bi, ti: (bi, ti, OFF_GK // GLA_KW)),
            pl.BlockSpec((1, t, GLA_VW), lambda bi, ti: (bi, ti, OFF_GV // GLA_VW)),
            pl.BlockSpec((1, t, GLA_VW), lambda bi, ti: (bi, ti, OFF_GO // GLA_VW)),
            pl.BlockSpec((1, t, LANES), lambda bi, ti: (bi, ti, 0)),
            pl.BlockSpec((LANES, GLA_KW), lambda bi, ti: (0, 0)),
            pl.BlockSpec((1, GLA_KW), lambda bi, ti: (0, 0)),
            pl.BlockSpec((1, GLA_DV), lambda bi, ti: (0, 0)),
        ],
        out_specs=pl.BlockSpec((1, t, GLA_VW), lambda bi, ti: (bi, ti, 0)),
        out_shape=jax.ShapeDtypeStruct((b, s, GLA_VW), BF16),
        scratch_shapes=[pltpu.VMEM((GLA_HEADS, GLA_DV, GLA_KW), F32)],
        compiler_params=pltpu.CompilerParams(
            dimension_semantics=("parallel", "arbitrary"), vmem_limit_bytes=VMEM_LIMIT),
        name="gla",
    )(proj3, proj3, proj3, proj3, lr3, w_gup, b_g, norm_w)


def _outproj_kernel(ya_ref, yl_ref, yg_ref, w_ref, x_ref, nw_ref, o_ref):
    y = jnp.concatenate([ya_ref[...], yl_ref[...], yg_ref[...]], axis=1)
    mix = jnp.dot(y, w_ref[...], preferred_element_type=F32)
    o_ref[...] = x_ref[...] + _rms(mix, nw_ref[...])


def _outproj(ya, yl, yg, w_out, x2, norm_w):
    m = x2.shape[0]
    tm = TM_OUT
    assert m % tm == 0
    return pl.pallas_call(
        _outproj_kernel,
        grid=(m // tm,),
        in_specs=[
            pl.BlockSpec((tm, DIFF_W), lambda i: (i, 0)),
            pl.BlockSpec((tm, LRU_W), lambda i: (i, 0)),
            pl.BlockSpec((tm, GLA_VW), lambda i: (i, 0)),
            pl.BlockSpec((D_MIX, D_MODEL), lambda i: (0, 0)),
            pl.BlockSpec((tm, D_MODEL), lambda i: (i, 0)),
            pl.BlockSpec((1, D_MODEL), lambda i: (0, 0)),
        ],
        out_specs=pl.BlockSpec((tm, D_MODEL), lambda i: (i, 0)),
        out_shape=jax.ShapeDtypeStruct((m, D_MODEL), F32),
        compiler_params=pltpu.CompilerParams(
            dimension_semantics=("parallel",), vmem_limit_bytes=VMEM_LIMIT),
        name="outproj",
    )(ya, yl, yg, w_out, x2, norm_w)


def _ffn_kernel(x_ref, nw1_ref, wg_ref, wu_ref, wd_ref, nw2_ref, o_ref, h_ref, acc_ref):
    f = pl.program_id(1)

    @pl.when(f == 0)
    def _():
        h_ref[...] = _rms(x_ref[...], nw1_ref[...]).astype(BF16)
        acc_ref[...] = jnp.zeros_like(acc_ref)

    h = h_ref[...]
    g = jnp.dot(h, wg_ref[...], preferred_element_type=F32)
    u = jnp.dot(h, wu_ref[...], preferred_element_type=F32)
    hid = (g * jax.nn.sigmoid(g) * u).astype(BF16)
    acc_ref[...] += jnp.dot(hid, wd_ref[...], preferred_element_type=F32)

    @pl.when(f == pl.num_programs(1) - 1)
    def _():
        o_ref[...] = x_ref[...] + _rms(acc_ref[...], nw2_ref[...])


def _ffn(x2, nw1, w_gate, w_up, w_down, nw2):
    m = x2.shape[0]
    tm, tf = TM_FFN, TF_FFN
    assert m % tm == 0 and D_FF % tf == 0
    return pl.pallas_call(
        _ffn_kernel,
        grid=(m // tm, D_FF // tf),
        in_specs=[
            pl.BlockSpec((tm, D_MODEL), lambda i, f: (i, 0)),
            pl.BlockSpec((1, D_MODEL), lambda i, f: (0, 0)),
            pl.BlockSpec((D_MODEL, tf), lambda i, f: (0, f)),
            pl.BlockSpec((D_MODEL, tf), lambda i, f: (0, f)),
            pl.BlockSpec((tf, D_MODEL), lambda i, f: (f, 0)),
            pl.BlockSpec((1, D_MODEL), lambda i, f: (0, 0)),
        ],
        out_specs=pl.BlockSpec((tm, D_MODEL), lambda i, f: (i, 0)),
        out_shape=jax.ShapeDtypeStruct((m, D_MODEL), F32),
        scratch_shapes=[pltpu.VMEM((tm, D_MODEL), BF16), pltpu.VMEM((tm, D_MODEL), F32)],
        compiler_params=pltpu.CompilerParams(
            dimension_semantics=("parallel", "arbitrary"), vmem_limit_bytes=VMEM_LIMIT),
        name="ffn",
    )(x2, nw1, w_gate, w_up, w_down, nw2)


def _rope_tables(seq):
    half = DIFF_HD // 2
    inv = ROPE_THETA ** (-jnp.arange(0, DIFF_HD, 2, dtype=F32) / DIFF_HD)
    ang = jnp.arange(seq, dtype=F32)[:, None] * inv[None, :]
    reps = LANES // half
    sign = jnp.tile(jnp.concatenate([-jnp.ones((half,), F32), jnp.ones((half,), F32)]), LANES // DIFF_HD)
    return jnp.stack([jnp.tile(jnp.cos(ang), (1, reps)), jnp.tile(jnp.sin(ang), (1, reps)) * sign[None, :]])


def _block_diag(w):
    n, c, d = w.shape
    eye = jnp.eye(n, dtype=w.dtype)
    return (eye[:, None, :, None] * w[:, :, None, :]).reshape(n * c, n * d)


@jax.jit
def _forward(x, pre_mix_norm, post_mix_norm, pre_ffn_norm, post_ffn_norm, w_in, w_out,
             lambda_q1, lambda_k1, lambda_q2, lambda_k2, diff_subln,
             conv_w, conv_b, w_rgate, b_rgate, w_igate, b_igate, lru_lambda,
             w_gla_gate_up, b_gla_gate, gla_norm, w_ffn_gate, w_ffn_up, w_ffn_down):
    b, s, dm = x.shape
    m = b * s
    rope_tab = _rope_tables(s)
    x2 = x.reshape(m, dm)
    row = lambda v: v.reshape(1, -1).astype(F32)
    for l in range(DEPTH):
        lam_init = 0.8 - 0.6 * math.exp(-0.3 * l)
        w_main = w_in[l, :, :N_MAIN].astype(BF16)
        w_lr = jnp.pad(w_in[l, :, N_MAIN:], ((0, 0), (0, LANES - GLA_RANK))).astype(BF16)
        proj, lr = _inproj(x2, row(pre_mix_norm[l]), w_main, w_lr)
        proj3 = proj.reshape(b, s, N_MAIN)
        lr3 = lr.reshape(b, s, LANES)

        lamv = jnp.pad(jnp.stack([lambda_q1[l], lambda_k1[l], lambda_q2[l], lambda_k2[l]]).astype(F32),
                       ((0, 0), (0, LANES - DIFF_HD)))
        y_attn = _diff_attention(proj3, rope_tab, lamv, row(diff_subln[l]), lam_init)

        w_gates = jnp.concatenate([_block_diag(w_rgate[l]), _block_diag(w_igate[l])], axis=1).astype(BF16)
        b_gates = jnp.concatenate([b_rgate[l], b_igate[l]]).reshape(1, -1).astype(F32)
        y_lru = _rglru(proj3, conv_w[l].astype(F32), row(conv_b[l]), w_gates, b_gates, row(lru_lambda[l]))

        w_gup = jnp.pad(w_gla_gate_up[l], ((0, LANES - GLA_RANK), (0, 0))).astype(BF16)
        y_gla = _gla(proj3, lr3, w_gup, row(b_gla_gate[l]), row(gla_norm[l]))

        x2 = _outproj(y_attn.reshape(m, DIFF_W), y_lru.reshape(m, LRU_W), y_gla.reshape(m, GLA_VW),
                      w_out[l].astype(BF16), x2, row(post_mix_norm[l]))
        x2 = _ffn(x2, row(pre_ffn_norm[l]), w_ffn_gate[l].astype(BF16), w_ffn_up[l].astype(BF16),
                  w_ffn_down[l].astype(BF16), row(post_ffn_norm[l]))
    return x2.reshape(b, s, dm)


def kernel(x, pre_mix_norm, post_mix_norm, pre_ffn_norm, post_ffn_norm, w_in, w_out, lambda_q1, lambda_k1, lambda_q2, lambda_k2, diff_subln, conv_w, conv_b, w_rgate, b_rgate, w_igate, b_igate, lru_lambda, w_gla_gate_up, b_gla_gate, gla_norm, w_ffn_gate, w_ffn_up, w_ffn_down):
    return _forward(x, pre_mix_norm, post_mix_norm, pre_ffn_norm, post_ffn_norm, w_in, w_out,
                    lambda_q1, lambda_k1, lambda_q2, lambda_k2, diff_subln,
                    conv_w, conv_b, w_rgate, b_rgate, w_igate, b_igate, lru_lambda,
                    w_gla_gate_up, b_gla_gate, gla_norm, w_ffn_gate, w_ffn_up, w_ffn_down)
```

```python
import functools
import math

import jax
import jax.numpy as jnp
from jax import lax
from jax.experimental import pallas as pl
from jax.experimental.pallas import tpu as pltpu

F32 = jnp.float32
BF16 = jnp.bfloat16

D_MODEL = 2048
DEPTH = 4
DIFF_HEADS = 8
DIFF_HD = 64
DIFF_VD = 2 * DIFF_HD
DIFF_W = DIFF_HEADS * DIFF_VD
ROPE_THETA = 10000.0
LRU_W = 512
LRU_BLOCKS = 8
CONV_W = 4
LRU_C = 8.0
GLA_HEADS = 4
GLA_DK = 64
GLA_DV = 128
GLA_KW = GLA_HEADS * GLA_DK
GLA_VW = GLA_HEADS * GLA_DV
GLA_RANK = 16
GLA_NORMALIZER = 16.0
GLA_CHUNK = 64
D_MIX = DIFF_W + LRU_W + GLA_VW
N_MAIN = 2 * DIFF_W + DIFF_W + 2 * LRU_W + 2 * GLA_KW + 2 * GLA_VW
D_FF = 5632
NORM_EPS = 1e-6

LANES = 128
VMEM_LIMIT = 56 * 1024 * 1024

OFF_Q, OFF_K, OFF_V = 0, DIFF_W, 2 * DIFF_W
OFF_LRU_G, OFF_LRU_X = 3 * DIFF_W, 3 * DIFF_W + LRU_W
OFF_GQ = OFF_LRU_X + LRU_W
OFF_GK = OFF_GQ + GLA_KW
OFF_GV = OFF_GK + GLA_KW
OFF_GO = OFF_GV + GLA_VW

TM_PROJ = 1024
TN_PROJ = 512
TQ_ATTN = 512
Q_SCALE = DIFF_HD ** -0.5 * math.log2(math.e)
T_LRU = 512
T_GLA = 256
TM_OUT = 512
TM_FFN = 512
TF_FFN = 512
TAIL = 16


def _rms(x, w):
    ms = jnp.mean(x * x, axis=-1, keepdims=True)
    return x * lax.rsqrt(ms + NORM_EPS) * w


def _inproj_kernel(x_ref, nw_ref, w_ref, wlr_ref, o_ref, lr_ref, h_ref):
    @pl.when(pl.program_id(1) == 0)
    def _():
        hb = _rms(x_ref[...], nw_ref[...]).astype(BF16)
        h_ref[...] = hb
        lr_ref[...] = jnp.dot(hb, wlr_ref[...], preferred_element_type=F32).astype(BF16)

    h = h_ref[...]
    half = o_ref.shape[1] // 2
    for c in range(2):
        cols = slice(c * half, (c + 1) * half)
        o_ref[:, cols] = jnp.dot(h, w_ref[:, cols], preferred_element_type=F32).astype(BF16)


def _inproj(x2, norm_w, w_in_b, w_lr, layer):
    m = x2.shape[0]
    tm, tn = TM_PROJ, TN_PROJ
    assert m % tm == 0 and N_MAIN % tn == 0
    return pl.pallas_call(
        _inproj_kernel,
        grid=(m // tm, N_MAIN // tn),
        in_specs=[
            pl.BlockSpec((tm, D_MODEL), lambda i, j: (i, 0)),
            pl.BlockSpec((1, D_MODEL), lambda i, j: (0, 0)),
            pl.BlockSpec((None, D_MODEL, tn), lambda i, j: (layer, 0, j)),
            pl.BlockSpec((D_MODEL, LANES), lambda i, j: (0, 0)),
        ],
        out_specs=[
            pl.BlockSpec((tm, tn), lambda i, j: (i, j)),
            pl.BlockSpec((tm, LANES), lambda i, j: (i, 0)),
        ],
        out_shape=[
            jax.ShapeDtypeStruct((m, N_MAIN), BF16),
            jax.ShapeDtypeStruct((m, LANES), BF16),
        ],
        scratch_shapes=[pltpu.VMEM((tm, D_MODEL), BF16)],
        compiler_params=pltpu.CompilerParams(
            dimension_semantics=("parallel", "arbitrary"), vmem_limit_bytes=VMEM_LIMIT),
        name="inproj",
    )(x2, norm_w, w_in_b, w_lr)


def _rope(a, cos, sin):
    lane = lax.broadcasted_iota(jnp.int32, a.shape, 1)
    lower_half = (lane & (DIFF_HD // 2)) == 0
    rot = jnp.where(lower_half,
                    pltpu.roll(a, LANES - DIFF_HD // 2, 1),
                    pltpu.roll(a, DIFF_HD // 2, 1))
    return a * cos + rot * sin


def _stack_maps(q):
    lane = lax.broadcasted_iota(jnp.int32, q.shape, 1)
    return jnp.concatenate([jnp.where(lane < DIFF_HD, q, 0.0),
                            jnp.where(lane >= DIFF_HD, q, 0.0)], axis=0).astype(BF16)


def _attn_kernel(q_ref, k_ref, v_ref, tab_ref, lamv_ref, sw_ref, o_ref,
                 kr_ref, vt_ref, sa_ref, sb_ref, qq_ref, qn_ref, p_ref, m_ref, l_ref, acc_ref,
                 *, tq, tk, lam_init):
    qi = pl.program_id(2)
    nq = pl.num_programs(2)

    def rotated_q(tile):
        rows = pl.ds(pl.multiple_of(tile * tq, tq), tq)
        q = _rope(q_ref[0, rows, :].astype(F32), tab_ref[0, rows, :], tab_ref[1, rows, :])
        return _stack_maps(q * Q_SCALE)

    def scores(qq, blk, s_ref):
        kb = kr_ref[pl.ds(pl.multiple_of(blk * tk, tk), tk), :]
        s_ref[...] = lax.dot_general(kb, qq, (((1,), (1,)), ((), ())), preferred_element_type=F32)

    def process(s_ref, blk, mask):
        s = s_ref[...]
        if mask is not None:
            s = jnp.where(mask, s, -jnp.inf)
        m = m_ref[...]
        m_new = jnp.maximum(m, jnp.max(s, axis=0, keepdims=True))
        alpha = jnp.exp2(m - m_new)
        m_ref[...] = m_new
        p_ref[...] = jnp.exp2((s - m_new).astype(BF16))
        pv = jnp.dot(vt_ref[blk], p_ref[...], preferred_element_type=F32)
        l_ref[...] = alpha * l_ref[...] + pv[DIFF_VD:DIFF_VD + 1]
        acc_ref[...] = alpha * acc_ref[...] + pv[:DIFF_VD]

    @pl.when(qi == 0)
    def _():
        kr_ref[...] = _rope(k_ref[0].astype(F32), tab_ref[0], tab_ref[1]).astype(BF16)
        ones = jnp.ones((16, tk), BF16)
        for j in range(vt_ref.shape[0]):
            vt = v_ref[0, j * tk:(j + 1) * tk, :].astype(F32).T.astype(BF16)
            vt_ref[j] = jnp.concatenate([vt, ones], axis=0)
        qq_ref[...] = rotated_q(0)
        scores(qq_ref[...], 0, sa_ref)

    @pl.when(qi > 0)
    def _():
        qq_ref[...] = qn_ref[...]

    m_ref[...] = jnp.full_like(m_ref, -jnp.inf)
    l_ref[...] = jnp.zeros_like(l_ref)
    acc_ref[...] = jnp.zeros_like(acc_ref)

    def pair(p, carry):
        qq = qq_ref[...]
        scores(qq, 2 * p + 1, sb_ref)
        process(sa_ref, 2 * p, None)
        scores(qq, 2 * p + 2, sa_ref)
        process(sb_ref, 2 * p + 1, None)
        return carry

    lax.fori_loop(0, qi, pair, 0)

    key = lax.broadcasted_iota(jnp.int32, (tk, 2 * tq), 0)
    qry = lax.broadcasted_iota(jnp.int32, (tk, 2 * tq), 1) & (tq - 1)
    d0 = 2 * qi
    scores(qq_ref[...], d0 + 1, sb_ref)
    process(sa_ref, d0, key <= qry)
    qn_ref[...] = rotated_q(jnp.minimum(qi + 1, nq - 1))
    scores(qn_ref[...], 0, sa_ref)
    process(sb_ref, d0 + 1, key + tk <= qry)

    o = acc_ref[...] / l_ref[...]
    lv = lamv_ref[...]
    lam = (jnp.exp(jnp.sum(lv[0:1] * lv[1:2], axis=-1, keepdims=True))
           - jnp.exp(jnp.sum(lv[2:3] * lv[3:4], axis=-1, keepdims=True)) + lam_init)
    d = o[:, :tq] - lam * o[:, tq:]
    ms = jnp.mean(d * d, axis=0, keepdims=True)
    y = (d * lax.rsqrt(ms + NORM_EPS)).T * (sw_ref[...] * (1.0 - lam_init))
    o_ref[0] = y.astype(BF16)


def _diff_attention(proj3, rope_tab, lamv, subln_w, lam_init):
    b, s, _ = proj3.shape
    tq = TQ_ATTN
    tk = tq // 2
    assert s % tq == 0 and tq & (tq - 1) == 0
    kb0, vb0 = OFF_K // LANES, OFF_V // LANES
    kern = functools.partial(_attn_kernel, tq=tq, tk=tk, lam_init=lam_init)
    return pl.pallas_call(
        kern,
        grid=(b, DIFF_HEADS, s // tq),
        in_specs=[
            pl.BlockSpec((1, s, LANES), lambda bi, h, qi: (bi, 0, h)),
            pl.BlockSpec((1, s, LANES), lambda bi, h, qi: (bi, 0, kb0 + h)),
            pl.BlockSpec((1, s, LANES), lambda bi, h, qi: (bi, 0, vb0 + h)),
            pl.BlockSpec((2, s, LANES), lambda bi, h, qi: (0, 0, 0)),
            pl.BlockSpec((4, LANES), lambda bi, h, qi: (0, 0)),
            pl.BlockSpec((1, DIFF_VD), lambda bi, h, qi: (0, 0)),
        ],
        out_specs=pl.BlockSpec((1, tq, LANES), lambda bi, h, qi: (bi, qi, h)),
        out_shape=jax.ShapeDtypeStruct((b, s, DIFF_W), BF16),
        scratch_shapes=[
            pltpu.VMEM((s, LANES), BF16), pltpu.VMEM((s // tk, DIFF_VD + 16, tk), BF16),
            pltpu.VMEM((tk, 2 * tq), F32), pltpu.VMEM((tk, 2 * tq), F32),
            pltpu.VMEM((2 * tq, LANES), BF16), pltpu.VMEM((2 * tq, LANES), BF16),
            pltpu.VMEM((tk, 2 * tq), BF16),
            pltpu.VMEM((1, 2 * tq), F32), pltpu.VMEM((1, 2 * tq), F32),
            pltpu.VMEM((DIFF_VD, 2 * tq), F32),
        ],
        compiler_params=pltpu.CompilerParams(
            dimension_semantics=("parallel", "parallel", "arbitrary"), vmem_limit_bytes=VMEM_LIMIT),
        name="diff_attn",
    )(proj3, proj3, proj3, rope_tab, lamv, subln_w)


def _shift_rows(x, d, fill):
    row = lax.broadcasted_iota(jnp.int32, x.shape, 0)
    return jnp.where(row >= d, pltpu.roll(x, d, 0), fill)


def _lru_kernel(g_ref, x_ref, tail_ref, cw_ref, cb_ref, wg_ref, bg_ref, lam_ref, o_ref, h_ref, *, t):
    ti = pl.program_id(1)

    @pl.when(ti == 0)
    def _():
        h_ref[...] = jnp.zeros_like(h_ref)

    xr = x_ref[0].astype(F32)
    tail = jnp.where(ti > 0, tail_ref[0].astype(F32), 0.0)
    ext = jnp.concatenate([tail, xr], axis=0)
    cw = cw_ref[...]
    xc = cb_ref[...] + cw[CONV_W - 1:CONV_W] * xr
    for d in range(1, CONV_W):
        xc = xc + cw[CONV_W - 1 - d:CONV_W - d] * pltpu.roll(ext, d, 0)[TAIL:]

    gates = jnp.dot(xc.astype(BF16), wg_ref[...], preferred_element_type=F32) + bg_ref[...]
    r = jax.nn.sigmoid(gates[:, :LRU_W])
    i = jax.nn.sigmoid(gates[:, LRU_W:])
    lam = lam_ref[...]
    softplus_neg_lam = jnp.maximum(-lam, 0.0) + jnp.log1p(jnp.exp(-jnp.abs(lam)))
    log_a = -LRU_C * r * softplus_neg_lam
    a = jnp.exp(log_a)
    u = jnp.sqrt(-jnp.tanh(log_a) * (a * a + 1.0)) * (i * xc)

    d = 1
    while d < t:
        u = a * _shift_rows(u, d, 0.0) + u
        a = a * _shift_rows(a, d, 1.0)
        d *= 2
    h = u + a * h_ref[...]
    h_ref[...] = h[t - 1:t]

    xg = g_ref[0].astype(F32)
    gelu = 0.5 * xg * (1.0 + jnp.tanh(math.sqrt(2.0 / math.pi) * (xg + 0.044715 * (xg * xg * xg))))
    o_ref[0] = (h * gelu).astype(BF16)


def _rglru(proj3, conv_w, conv_b, w_gates, b_gates, lru_lambda):
    b, s, _ = proj3.shape
    t = T_LRU
    assert s % t == 0 and t % TAIL == 0
    gb, xb = OFF_LRU_G // LRU_W, OFF_LRU_X // LRU_W
    kern = functools.partial(_lru_kernel, t=t)
    return pl.pallas_call(
        kern,
        grid=(b, s // t),
        in_specs=[
            pl.BlockSpec((1, t, LRU_W), lambda bi, ti: (bi, ti, gb)),
            pl.BlockSpec((1, t, LRU_W), lambda bi, ti: (bi, ti, xb)),
            pl.BlockSpec((1, TAIL, LRU_W), lambda bi, ti: (bi, jnp.maximum(ti * (t // TAIL) - 1, 0), xb)),
            pl.BlockSpec((CONV_W, LRU_W), lambda bi, ti: (0, 0)),
            pl.BlockSpec((1, LRU_W), lambda bi, ti: (0, 0)),
            pl.BlockSpec((LRU_W, 2 * LRU_W), lambda bi, ti: (0, 0)),
            pl.BlockSpec((1, 2 * LRU_W), lambda bi, ti: (0, 0)),
            pl.BlockSpec((1, LRU_W), lambda bi, ti: (0, 0)),
        ],
        out_specs=pl.BlockSpec((1, t, LRU_W), lambda bi, ti: (bi, ti, 0)),
        out_shape=jax.ShapeDtypeStruct((b, s, LRU_W), BF16),
        scratch_shapes=[pltpu.VMEM((1, LRU_W), F32)],
        compiler_params=pltpu.CompilerParams(
            dimension_semantics=("parallel", "arbitrary"), vmem_limit_bytes=VMEM_LIMIT),
        name="rglru",
    )(proj3, proj3, proj3, conv_w, conv_b, w_gates, b_gates, lru_lambda)


def _gla_kernel(q_ref, k_ref, v_ref, go_ref, lr_ref, wg_ref, bg_ref, nw_ref, o_ref, st_ref, *, t):
    ti = pl.program_id(1)

    @pl.when(ti == 0)
    def _():
        st_ref[...] = jnp.zeros_like(st_ref)

    c = GLA_CHUNK
    n = t // c
    z = jnp.dot(lr_ref[0], wg_ref[...], preferred_element_type=F32) + bg_ref[...]
    gk = (jnp.minimum(z, 0.0) - jnp.log1p(jnp.exp(-jnp.abs(z)))) * (1.0 / GLA_NORMALIZER)

    row = lax.broadcasted_iota(jnp.int32, (t, GLA_KW), 0)
    rc = row & (c - 1)
    bc = gk
    d = 1
    while d < c:
        bc = bc + jnp.where(rc >= d, pltpu.roll(bc, d, 0), 0.0)
        d *= 2
    bl3 = jnp.broadcast_to(bc.reshape(n, c, GLA_KW)[:, c - 1:c, :], (n, c, GLA_KW))
    bl = bl3.reshape(t, GLA_KW)

    q = q_ref[0].astype(F32) * (GLA_DK ** -0.5)
    k = k_ref[0].astype(F32)
    qe = q * jnp.exp(bc)
    ke = k * jnp.exp(-bc)
    kd = k * jnp.exp(bl - bc)
    dec = jnp.exp(bl)

    lane = lax.broadcasted_iota(jnp.int32, (t, GLA_KW), 1)
    col = lax.broadcasted_iota(jnp.int32, (t, t), 1)
    rowt = lax.broadcasted_iota(jnp.int32, (t, t), 0)
    causal = (col <= rowt) & ((col & -c) == (rowt & -c))
    qe_b = qe.astype(BF16)
    nw = nw_ref[...]
    for h in range(GLA_HEADS):
        in_head = (lane >= h * GLA_DK) & (lane < (h + 1) * GLA_DK)
        ke_h = jnp.where(in_head, ke, 0.0).astype(BF16)
        kd_h = jnp.where(in_head, kd, 0.0).astype(BF16)
        v_h = v_ref[0, :, h * GLA_DV:(h + 1) * GLA_DV]
        att = lax.dot_general(qe_b, ke_h, (((1,), (1,)), ((), ())), preferred_element_type=F32)
        att = jnp.where(causal, att, 0.0)
        o_intra = jnp.dot(att.astype(BF16), v_h, preferred_element_type=F32)
        st = st_ref[h]
        o_inter = []
        for ci in range(n):
            sl = slice(ci * c, (ci + 1) * c)
            o_inter.append(lax.dot_general(qe_b[sl], st.astype(BF16), (((1,), (1,)), ((), ())),
                                           preferred_element_type=F32))
            kvt = lax.dot_general(v_h[sl], kd_h[sl], (((0,), (0,)), ((), ())),
                                  preferred_element_type=F32)
            st = st * dec[ci * c:ci * c + 1] + kvt
        st_ref[h] = st
        o = o_intra + jnp.concatenate(o_inter, axis=0)
        go = go_ref[0, :, h * GLA_DV:(h + 1) * GLA_DV].astype(F32)
        o_ref[0, :, h * GLA_DV:(h + 1) * GLA_DV] = (_rms(o, nw) * (go * jax.nn.sigmoid(go))).astype(BF16)


def _gla(proj3, lr3, w_gup, b_g, norm_w):
    b, s, _ = proj3.shape
    t = T_GLA
    assert s % t == 0 and t % GLA_CHUNK == 0
    kern = functools.partial(_gla_kernel, t=t)
    return pl.pallas_call(
        kern,
        grid=(b, s // t),
        in_specs=[
            pl.BlockSpec((1, t, GLA_KW), lambda bi, ti: (bi, ti, OFF_GQ // GLA_KW)),
            pl.BlockSpec((1, t, GLA_KW), lambda bi, ti: (bi, ti, OFF_GK // GLA_KW)),
            pl.BlockSpec((1, t, GLA_VW), lambda bi, ti: (bi, ti, OFF_GV // GLA_VW)),
            pl.BlockSpec((1, t, GLA_VW), lambda bi, ti: (bi, ti, OFF_GO // GLA_VW)),
            pl.BlockSpec((1, t, LANES), lambda bi, ti: (bi, ti, 0)),
            pl.BlockSpec((LANES, GLA_KW), lambda bi, ti: (0, 0)),
            pl.BlockSpec((1, GLA_KW), lambda bi, ti: (0, 0)),
            pl.BlockSpec((1, GLA_DV), lambda bi, ti: (0, 0)),
        ],
        out_specs=pl.BlockSpec((1, t, GLA_VW), lambda bi, ti: (bi, ti, 0)),
        out_shape=jax.ShapeDtypeStruct((b, s, GLA_VW), BF16),
        scratch_shapes=[pltpu.VMEM((GLA_HEADS, GLA_DV, GLA_KW), F32)],
        compiler_params=pltpu.CompilerParams(
            dimension_semantics=("parallel", "arbitrary"), vmem_limit_bytes=VMEM_LIMIT),
        name="gla",
    )(proj3, proj3, proj3, proj3, lr3, w_gup, b_g, norm_w)


def _outproj_kernel(ya_ref, yl_ref, yg_ref, w_ref, x_ref, nw_ref, o_ref):
    y = jnp.concatenate([ya_ref[...], yl_ref[...], yg_ref[...]], axis=1)
    mix = jnp.dot(y, w_ref[...], preferred_element_type=F32)
    o_ref[...] = x_ref[...] + _rms(mix, nw_ref[...])


def _outproj(ya, yl, yg, w_out_b, layer, x2, norm_w):
    m = x2.shape[0]
    tm = TM_OUT
    assert m % tm == 0
    return pl.pallas_call(
        _outproj_kernel,
        grid=(m // tm,),
        in_specs=[
            pl.BlockSpec((tm, DIFF_W), lambda i: (i, 0)),
            pl.BlockSpec((tm, LRU_W), lambda i: (i, 0)),
            pl.BlockSpec((tm, GLA_VW), lambda i: (i, 0)),
            pl.BlockSpec((None, D_MIX, D_MODEL), lambda i: (layer, 0, 0)),
            pl.BlockSpec((tm, D_MODEL), lambda i: (i, 0)),
            pl.BlockSpec((1, D_MODEL), lambda i: (0, 0)),
        ],
        out_specs=pl.BlockSpec((tm, D_MODEL), lambda i: (i, 0)),
        out_shape=jax.ShapeDtypeStruct((m, D_MODEL), F32),
        compiler_params=pltpu.CompilerParams(
            dimension_semantics=("parallel",), vmem_limit_bytes=VMEM_LIMIT),
        name="outproj",
    )(ya, yl, yg, w_out_b, x2, norm_w)


def _ffn_kernel(x_ref, nw1_ref, wg_ref, wu_ref, wd_ref, nw2_ref, o_ref, h_ref, acc_ref):
    f = pl.program_id(1)

    @pl.when(f == 0)
    def _():
        h_ref[...] = _rms(x_ref[...], nw1_ref[...]).astype(BF16)
        acc_ref[...] = jnp.zeros_like(acc_ref)

    h = h_ref[...]
    g = jnp.dot(h, wg_ref[...], preferred_element_type=F32)
    u = jnp.dot(h, wu_ref[...], preferred_element_type=F32)
    hid = (g * jax.nn.sigmoid(g) * u).astype(BF16)
    acc_ref[...] += jnp.dot(hid, wd_ref[...], preferred_element_type=F32)

    @pl.when(f == pl.num_programs(1) - 1)
    def _():
        o_ref[...] = x_ref[...] + _rms(acc_ref[...], nw2_ref[...])


def _ffn(x2, nw1, w_gate_b, w_up_b, w_down_b, layer, nw2):
    m = x2.shape[0]
    tm, tf = TM_FFN, TF_FFN
    assert m % tm == 0 and D_FF % tf == 0
    return pl.pallas_call(
        _ffn_kernel,
        grid=(m // tm, D_FF // tf),
        in_specs=[
            pl.BlockSpec((tm, D_MODEL), lambda i, f: (i, 0)),
            pl.BlockSpec((1, D_MODEL), lambda i, f: (0, 0)),
            pl.BlockSpec((None, D_MODEL, tf), lambda i, f: (layer, 0, f)),
            pl.BlockSpec((None, D_MODEL, tf), lambda i, f: (layer, 0, f)),
            pl.BlockSpec((None, tf, D_MODEL), lambda i, f: (layer, f, 0)),
            pl.BlockSpec((1, D_MODEL), lambda i, f: (0, 0)),
        ],
        out_specs=pl.BlockSpec((tm, D_MODEL), lambda i, f: (i, 0)),
        out_shape=jax.ShapeDtypeStruct((m, D_MODEL), F32),
        scratch_shapes=[pltpu.VMEM((tm, D_MODEL), BF16), pltpu.VMEM((tm, D_MODEL), F32)],
        compiler_params=pltpu.CompilerParams(
            dimension_semantics=("parallel", "arbitrary"), vmem_limit_bytes=VMEM_LIMIT),
        name="ffn",
    )(x2, nw1, w_gate_b, w_up_b, w_down_b, nw2)


def _rope_tables(seq):
    half = DIFF_HD // 2
    inv = ROPE_THETA ** (-jnp.arange(0, DIFF_HD, 2, dtype=F32) / DIFF_HD)
    ang = jnp.arange(seq, dtype=F32)[:, None] * inv[None, :]
    reps = LANES // half
    sign = jnp.tile(jnp.concatenate([-jnp.ones((half,), F32), jnp.ones((half,), F32)]), LANES // DIFF_HD)
    return jnp.stack([jnp.tile(jnp.cos(ang), (1, reps)), jnp.tile(jnp.sin(ang), (1, reps)) * sign[None, :]])


def _block_diag(w):
    n, c, d = w.shape
    eye = jnp.eye(n, dtype=w.dtype)
    return (eye[:, None, :, None] * w[:, :, None, :]).reshape(n * c, n * d)


@jax.jit
def _forward(x, pre_mix_norm, post_mix_norm, pre_ffn_norm, post_ffn_norm, w_in, w_out,
             lambda_q1, lambda_k1, lambda_q2, lambda_k2, diff_subln,
             conv_w, conv_b, w_rgate, b_rgate, w_igate, b_igate, lru_lambda,
             w_gla_gate_up, b_gla_gate, gla_norm, w_ffn_gate, w_ffn_up, w_ffn_down):
    b, s, dm = x.shape
    m = b * s
    rope_tab = _rope_tables(s)
    x2 = x.reshape(m, dm)
    row = lambda v: v.reshape(1, -1).astype(F32)
    w_in_b, w_out_b = w_in.astype(BF16), w_out.astype(BF16)
    w_gate_b, w_up_b, w_down_b = w_ffn_gate.astype(BF16), w_ffn_up.astype(BF16), w_ffn_down.astype(BF16)
    for l in range(DEPTH):
        lam_init = 0.8 - 0.6 * math.exp(-0.3 * l)
        w_lr = jnp.pad(w_in[l, :, N_MAIN:], ((0, 0), (0, LANES - GLA_RANK))).astype(BF16)
        proj, lr = _inproj(x2, row(pre_mix_norm[l]), w_in_b, w_lr, l)
        proj3 = proj.reshape(b, s, N_MAIN)
        lr3 = lr.reshape(b, s, LANES)

        lamv = jnp.pad(jnp.stack([lambda_q1[l], lambda_k1[l], lambda_q2[l], lambda_k2[l]]).astype(F32),
                       ((0, 0), (0, LANES - DIFF_HD)))
        y_attn = _diff_attention(proj3, rope_tab, lamv, row(diff_subln[l]), lam_init)

        w_gates = jnp.concatenate([_block_diag(w_rgate[l]), _block_diag(w_igate[l])], axis=1).astype(BF16)
        b_gates = jnp.concatenate([b_rgate[l], b_igate[l]]).reshape(1, -1).astype(F32)
        y_lru = _rglru(proj3, conv_w[l].astype(F32), row(conv_b[l]), w_gates, b_gates, row(lru_lambda[l]))

        w_gup = jnp.pad(w_gla_gate_up[l], ((0, LANES - GLA_RANK), (0, 0))).astype(BF16)
        y_gla = _gla(proj3, lr3, w_gup, row(b_gla_gate[l]), row(gla_norm[l]))

        x2 = _outproj(y_attn.reshape(m, DIFF_W), y_lru.reshape(m, LRU_W), y_gla.reshape(m, GLA_VW),
                      w_out_b, l, x2, row(post_mix_norm[l]))
        x2 = _ffn(x2, row(pre_ffn_norm[l]), w_gate_b, w_up_b, w_down_b, l, row(post_ffn_norm[l]))
    return x2.reshape(b, s, dm)


def kernel(x, pre_mix_norm, post_mix_norm, pre_ffn_norm, post_ffn_norm, w_in, w_out, lambda_q1, lambda_k1, lambda_q2, lambda_k2, diff_subln, conv_w, conv_b, w_rgate, b_rgate, w_igate, b_igate, lru_lambda, w_gla_gate_up, b_gla_gate, gla_norm, w_ffn_gate, w_ffn_up, w_ffn_down):
    return _forward(x, pre_mix_norm, post_mix_norm, pre_ffn_norm, post_ffn_norm, w_in, w_out,
                    lambda_q1, lambda_k1, lambda_q2, lambda_k2, diff_subln,
                    conv_w, conv_b, w_rgate, b_rgate, w_igate, b_igate, lru_lambda,
                    w_gla_gate_up, b_gla_gate, gla_norm, w_ffn_gate, w_ffn_up, w_ffn_down)
```

```python
import functools
import math

import jax
import jax.numpy as jnp
from jax import lax
from jax.experimental import pallas as pl
from jax.experimental.pallas import tpu as pltpu

F32 = jnp.float32
BF16 = jnp.bfloat16

D_MODEL = 2048
DEPTH = 4
DIFF_HEADS = 8
DIFF_HD = 64
DIFF_VD = 2 * DIFF_HD
DIFF_W = DIFF_HEADS * DIFF_VD
ROPE_THETA = 10000.0
LRU_W = 512
LRU_BLOCKS = 8
CONV_W = 4
LRU_C = 8.0
GLA_HEADS = 4
GLA_DK = 64
GLA_DV = 128
GLA_KW = GLA_HEADS * GLA_DK
GLA_VW = GLA_HEADS * GLA_DV
GLA_RANK = 16
GLA_NORMALIZER = 16.0
GLA_CHUNK = 64
D_MIX = DIFF_W + LRU_W + GLA_VW
N_MAIN = 2 * DIFF_W + DIFF_W + 2 * LRU_W + 2 * GLA_KW + 2 * GLA_VW
D_FF = 5632
NORM_EPS = 1e-6

LANES = 128
SUBLANES = 8
VMEM_LIMIT = 56 * 1024 * 1024

OFF_Q, OFF_K, OFF_V = 0, DIFF_W, 2 * DIFF_W
OFF_LRU_G, OFF_LRU_X = 3 * DIFF_W, 3 * DIFF_W + LRU_W
OFF_GQ = OFF_LRU_X + LRU_W
OFF_GK = OFF_GQ + GLA_KW
OFF_GV = OFF_GK + GLA_KW
OFF_GO = OFF_GV + GLA_VW

TM_PROJ = 1024
TN_PROJ = 512
TQ_ATTN = 512
Q_SCALE = DIFF_HD ** -0.5 * math.log2(math.e)
T_LRU = 512
T_GLA = 256
TM_OUT = 512
TM_FFN = 512
TF_FFN = 512
TAIL = 16


def _rms(x, w):
    ms = jnp.mean(x * x, axis=-1, keepdims=True)
    return x * lax.rsqrt(ms + NORM_EPS) * w


def _inproj_kernel(x_ref, nw_ref, w_ref, wlr_ref, o_ref, lr_ref, h_ref):
    @pl.when(pl.program_id(1) == 0)
    def _():
        hb = _rms(x_ref[...], nw_ref[...]).astype(BF16)
        h_ref[...] = hb
        lr_ref[...] = jnp.dot(hb, wlr_ref[...], preferred_element_type=F32).astype(BF16)

    h = h_ref[...]
    half = o_ref.shape[1] // 2
    for c in range(2):
        cols = slice(c * half, (c + 1) * half)
        o_ref[:, cols] = jnp.dot(h, w_ref[:, cols], preferred_element_type=F32).astype(BF16)


def _inproj(x2, norm_w, w_in_b, w_lr, layer):
    m = x2.shape[0]
    tm, tn = TM_PROJ, TN_PROJ
    assert m % tm == 0 and N_MAIN % tn == 0
    return pl.pallas_call(
        _inproj_kernel,
        grid=(m // tm, N_MAIN // tn),
        in_specs=[
            pl.BlockSpec((tm, D_MODEL), lambda i, j: (i, 0)),
            pl.BlockSpec((1, D_MODEL), lambda i, j: (0, 0)),
            pl.BlockSpec((None, D_MODEL, tn), lambda i, j: (layer, 0, j)),
            pl.BlockSpec((D_MODEL, LANES), lambda i, j: (0, 0)),
        ],
        out_specs=[
            pl.BlockSpec((tm, tn), lambda i, j: (i, j)),
            pl.BlockSpec((tm, LANES), lambda i, j: (i, 0)),
        ],
        out_shape=[
            jax.ShapeDtypeStruct((m, N_MAIN), BF16),
            jax.ShapeDtypeStruct((m, LANES), BF16),
        ],
        scratch_shapes=[pltpu.VMEM((tm, D_MODEL), BF16)],
        compiler_params=pltpu.CompilerParams(
            dimension_semantics=("parallel", "arbitrary"), vmem_limit_bytes=VMEM_LIMIT),
        name="inproj",
    )(x2, norm_w, w_in_b, w_lr)


def _rope(a, cos, sin):
    lane = lax.broadcasted_iota(jnp.int32, a.shape, 1)
    lower_half = (lane & (DIFF_HD // 2)) == 0
    rot = jnp.where(lower_half,
                    pltpu.roll(a, LANES - DIFF_HD // 2, 1),
                    pltpu.roll(a, DIFF_HD // 2, 1))
    return a * cos + rot * sin


def _stack_maps(q):
    lane = lax.broadcasted_iota(jnp.int32, q.shape, 1)
    stacked = jnp.concatenate([jnp.where(lane < DIFF_HD, q, 0.0),
                               jnp.where(lane >= DIFF_HD, q, 0.0)], axis=0)
    return stacked.T.astype(BF16)


def _attn_kernel(q_ref, k_ref, v_ref, tab_ref, lamv_ref, sw_ref, o_ref,
                 kr_ref, vt_ref, sa_ref, sb_ref, qq_ref, qn_ref, p_ref, m_ref, l_ref, acc_ref,
                 *, tq, tk, lam_init):
    nq = q_ref.shape[1] // tq

    def rotated_q(tile):
        rows = pl.ds(pl.multiple_of(tile * tq, tq), tq)
        q = _rope(q_ref[0, rows, :].astype(F32), tab_ref[0, rows, :], tab_ref[1, rows, :])
        return _stack_maps(q * Q_SCALE)

    def scores(qq, blk, s_ref):
        kb = kr_ref[pl.ds(pl.multiple_of(blk * tk, tk), tk), :]
        s_ref[...] = jnp.dot(kb, qq, preferred_element_type=F32)

    def process(s_ref, blk, mask):
        s = s_ref[...]
        if mask is not None:
            s = jnp.where(mask, s, -jnp.inf)
        m = m_ref[...]
        m_new = jnp.maximum(m, jnp.max(s, axis=0, keepdims=True))
        alpha = jnp.exp2(m - m_new)
        m_ref[...] = m_new
        p_ref[...] = jnp.exp2(s - m_new).astype(BF16)
        pv = jnp.dot(vt_ref[blk], p_ref[...], preferred_element_type=F32)
        l_ref[...] = alpha * l_ref[...] + pv[DIFF_VD:DIFF_VD + 1]
        acc_ref[...] = alpha * acc_ref[...] + pv[:DIFF_VD]

    kr_ref[...] = _rope(k_ref[0].astype(F32), tab_ref[0], tab_ref[1]).astype(BF16)
    ones = jnp.ones((16, tk), BF16)
    for j in range(vt_ref.shape[0]):
        vt = v_ref[0, j * tk:(j + 1) * tk, :].astype(F32).T.astype(BF16)
        vt_ref[j] = jnp.concatenate([vt, ones], axis=0)
    qn_ref[...] = rotated_q(0)
    scores(qn_ref[...], 0, sa_ref)

    lv = lamv_ref[...]
    lam = (jnp.exp(jnp.sum(lv[0:1] * lv[1:2], axis=-1, keepdims=True))
           - jnp.exp(jnp.sum(lv[2:3] * lv[3:4], axis=-1, keepdims=True)) + lam_init)

    def pair(p):
        qq = qq_ref[...]
        scores(qq, 2 * p + 1, sb_ref)
        process(sa_ref, 2 * p, None)
        scores(qq, 2 * p + 2, sa_ref)
        process(sb_ref, 2 * p + 1, None)

    def two_pairs(j, first):
        pair(first + 2 * j)
        pair(first + 2 * j + 1)
        return first

    def tile(qi, carry):
        qq_ref[...] = qn_ref[...]
        m_ref[...] = jnp.full_like(m_ref, -jnp.inf)
        l_ref[...] = jnp.zeros_like(l_ref)
        acc_ref[...] = jnp.zeros_like(acc_ref)
        odd = qi & 1

        @pl.when(odd == 1)
        def _():
            pair(0)

        lax.fori_loop(0, qi // 2, two_pairs, odd)

        key = lax.broadcasted_iota(jnp.int32, (tk, 2 * tq), 0)
        qry = lax.broadcasted_iota(jnp.int32, (tk, 2 * tq), 1) & (tq - 1)
        d0 = 2 * qi
        scores(qq_ref[...], d0 + 1, sb_ref)
        process(sa_ref, d0, key <= qry)
        qn_ref[...] = rotated_q(jnp.minimum(qi + 1, nq - 1))
        scores(qn_ref[...], 0, sa_ref)
        process(sb_ref, d0 + 1, key + tk <= qry)

        o = acc_ref[...] / l_ref[...]
        d = o[:, :tq] - lam * o[:, tq:]
        ms = jnp.mean(d * d, axis=0, keepdims=True)
        y = (d * lax.rsqrt(ms + NORM_EPS)).T * (sw_ref[...] * (1.0 - lam_init))
        o_ref[0, pl.ds(pl.multiple_of(qi * tq, tq), tq), :] = y.astype(BF16)
        return carry

    lax.fori_loop(0, nq, tile, 0)


def _diff_attention(proj3, rope_tab, lamv, subln_w, lam_init):
    b, s, _ = proj3.shape
    tq = TQ_ATTN
    tk = tq // 2
    assert s % tq == 0 and tq & (tq - 1) == 0
    kb0, vb0 = OFF_K // LANES, OFF_V // LANES
    kern = functools.partial(_attn_kernel, tq=tq, tk=tk, lam_init=lam_init)
    return pl.pallas_call(
        kern,
        grid=(b, DIFF_HEADS),
        in_specs=[
            pl.BlockSpec((1, s, LANES), lambda bi, h: (bi, 0, h)),
            pl.BlockSpec((1, s, LANES), lambda bi, h: (bi, 0, kb0 + h)),
            pl.BlockSpec((1, s, LANES), lambda bi, h: (bi, 0, vb0 + h)),
            pl.BlockSpec((2, s, LANES), lambda bi, h: (0, 0, 0)),
            pl.BlockSpec((4, LANES), lambda bi, h: (0, 0)),
            pl.BlockSpec((1, DIFF_VD), lambda bi, h: (0, 0)),
        ],
        out_specs=pl.BlockSpec((1, s, LANES), lambda bi, h: (bi, 0, h)),
        out_shape=jax.ShapeDtypeStruct((b, s, DIFF_W), BF16),
        scratch_shapes=[
            pltpu.VMEM((s, LANES), BF16), pltpu.VMEM((s // tk, DIFF_VD + 16, tk), BF16),
            pltpu.VMEM((tk, 2 * tq), F32), pltpu.VMEM((tk, 2 * tq), F32),
            pltpu.VMEM((LANES, 2 * tq), BF16), pltpu.VMEM((LANES, 2 * tq), BF16),
            pltpu.VMEM((tk, 2 * tq), BF16),
            pltpu.VMEM((1, 2 * tq), F32), pltpu.VMEM((1, 2 * tq), F32),
            pltpu.VMEM((DIFF_VD, 2 * tq), F32),
        ],
        compiler_params=pltpu.CompilerParams(
            dimension_semantics=("parallel", "parallel"), vmem_limit_bytes=VMEM_LIMIT),
        name="diff_attn",
    )(proj3, proj3, proj3, rope_tab, lamv, subln_w)


def _lru_kernel(g_ref, x_ref, tail_ref, cw_ref, cb_ref, wg_ref, bg_ref, lam_ref, o_ref, h_ref, *, t):
    ti = pl.program_id(1)

    @pl.when(ti == 0)
    def _():
        h_ref[...] = jnp.zeros_like(h_ref)

    xr = x_ref[0].astype(F32)
    tail = jnp.where(ti > 0, tail_ref[0].astype(F32), 0.0)
    ext = jnp.concatenate([tail, xr], axis=0)
    cw = cw_ref[...]
    xc = cb_ref[...] + cw[CONV_W - 1:CONV_W] * xr
    for d in range(1, CONV_W):
        xc = xc + cw[CONV_W - 1 - d:CONV_W - d] * pltpu.roll(ext, d, 0)[TAIL:]

    gates = jnp.dot(xc.astype(BF16), wg_ref[...], preferred_element_type=F32) + bg_ref[...]
    r = jax.nn.sigmoid(gates[:, :LRU_W])
    i = jax.nn.sigmoid(gates[:, LRU_W:])
    lam = lam_ref[...]
    softplus_neg_lam = jnp.maximum(-lam, 0.0) + jnp.log1p(jnp.exp(-jnp.abs(lam)))
    log_a = -LRU_C * r * softplus_neg_lam
    a = jnp.exp(log_a)
    u = jnp.sqrt(-jnp.tanh(log_a) * (a * a + 1.0)) * (i * xc)

    in_group = lax.broadcasted_iota(jnp.int32, (t // SUBLANES, SUBLANES, LRU_W), 1)
    u = u.reshape(t // SUBLANES, SUBLANES, LRU_W)
    a = a.reshape(t // SUBLANES, SUBLANES, LRU_W)
    d = 1
    while d < SUBLANES:
        keep = in_group >= d
        u = a * jnp.where(keep, pltpu.roll(u, d, 1), 0.0) + u
        a = a * jnp.where(keep, pltpu.roll(a, d, 1), 1.0)
        d *= 2
    u = u.reshape(t, LRU_W)
    a = a.reshape(t, LRU_W)
    carry = h_ref[...]
    groups = []
    for g in range(t // SUBLANES):
        rows = slice(g * SUBLANES, (g + 1) * SUBLANES)
        hg = u[rows] + a[rows] * carry
        groups.append(hg)
        carry = hg[SUBLANES - 1:SUBLANES]
    h = jnp.concatenate(groups, axis=0)
    h_ref[...] = carry

    xg = g_ref[0].astype(F32)
    gelu = 0.5 * xg * (1.0 + jnp.tanh(math.sqrt(2.0 / math.pi) * (xg + 0.044715 * (xg * xg * xg))))
    o_ref[0] = (h * gelu).astype(BF16)


def _rglru(proj3, conv_w, conv_b, w_gates, b_gates, lru_lambda):
    b, s, _ = proj3.shape
    t = T_LRU
    assert s % t == 0 and t % TAIL == 0
    gb, xb = OFF_LRU_G // LRU_W, OFF_LRU_X // LRU_W
    kern = functools.partial(_lru_kernel, t=t)
    return pl.pallas_call(
        kern,
        grid=(b, s // t),
        in_specs=[
            pl.BlockSpec((1, t, LRU_W), lambda bi, ti: (bi, ti, gb)),
            pl.BlockSpec((1, t, LRU_W), lambda bi, ti: (bi, ti, xb)),
            pl.BlockSpec((1, TAIL, LRU_W), lambda bi, ti: (bi, jnp.maximum(ti * (t // TAIL) - 1, 0), xb)),
            pl.BlockSpec((CONV_W, LRU_W), lambda bi, ti: (0, 0)),
            pl.BlockSpec((1, LRU_W), lambda bi, ti: (0, 0)),
            pl.BlockSpec((LRU_W, 2 * LRU_W), lambda bi, ti: (0, 0)),
            pl.BlockSpec((1, 2 * LRU_W), lambda bi, ti: (0, 0)),
            pl.BlockSpec((1, LRU_W), lambda bi, ti: (0, 0)),
        ],
        out_specs=pl.BlockSpec((1, t, LRU_W), lambda bi, ti: (bi, ti, 0)),
        out_shape=jax.ShapeDtypeStruct((b, s, LRU_W), BF16),
        scratch_shapes=[pltpu.VMEM((1, LRU_W), F32)],
        compiler_params=pltpu.CompilerParams(
            dimension_semantics=("parallel", "arbitrary"), vmem_limit_bytes=VMEM_LIMIT),
        name="rglru",
    )(proj3, proj3, proj3, conv_w, conv_b, w_gates, b_gates, lru_lambda)


def _gla_kernel(q_ref, k_ref, v_ref, go_ref, lr_ref, wg_ref, bg_ref, nw_ref, o_ref, st_ref, *, t):
    ti = pl.program_id(1)

    @pl.when(ti == 0)
    def _():
        st_ref[...] = jnp.zeros_like(st_ref)

    c = GLA_CHUNK
    n = t // c
    z = jnp.dot(lr_ref[0], wg_ref[...], preferred_element_type=F32) + bg_ref[...]
    gk = (jnp.minimum(z, 0.0) - jnp.log1p(jnp.exp(-jnp.abs(z)))) * (1.0 / GLA_NORMALIZER)

    row = lax.broadcasted_iota(jnp.int32, (t, GLA_KW), 0)
    rc = row & (c - 1)
    bc = gk
    d = 1
    while d < c:
        bc = bc + jnp.where(rc >= d, pltpu.roll(bc, d, 0), 0.0)
        d *= 2
    bl3 = jnp.broadcast_to(bc.reshape(n, c, GLA_KW)[:, c - 1:c, :], (n, c, GLA_KW))
    bl = bl3.reshape(t, GLA_KW)

    q = q_ref[0].astype(F32) * (GLA_DK ** -0.5)
    k = k_ref[0].astype(F32)
    qe = q * jnp.exp(bc)
    ke = k * jnp.exp(-bc)
    kd = k * jnp.exp(bl - bc)
    dec = jnp.exp(bl)

    lane = lax.broadcasted_iota(jnp.int32, (t, GLA_KW), 1)
    col = lax.broadcasted_iota(jnp.int32, (t, t), 1)
    rowt = lax.broadcasted_iota(jnp.int32, (t, t), 0)
    causal = (col <= rowt) & ((col & -c) == (rowt & -c))
    qe_b = qe.astype(BF16)
    nw = nw_ref[...]
    for h in range(GLA_HEADS):
        in_head = (lane >= h * GLA_DK) & (lane < (h + 1) * GLA_DK)
        ke_h = jnp.where(in_head, ke, 0.0).astype(BF16)
        kd_h = jnp.where(in_head, kd, 0.0).astype(BF16)
        v_h = v_ref[0, :, h * GLA_DV:(h + 1) * GLA_DV]
        att = lax.dot_general(qe_b, ke_h, (((1,), (1,)), ((), ())), preferred_element_type=F32)
        att = jnp.where(causal, att, 0.0)
        o_intra = jnp.dot(att.astype(BF16), v_h, preferred_element_type=F32)
        st = st_ref[h]
        o_inter = []
        for ci in range(n):
            sl = slice(ci * c, (ci + 1) * c)
            o_inter.append(lax.dot_general(qe_b[sl], st.astype(BF16), (((1,), (1,)), ((), ())),
                                           preferred_element_type=F32))
            kvt = lax.dot_general(v_h[sl], kd_h[sl], (((0,), (0,)), ((), ())),
                                  preferred_element_type=F32)
            st = st * dec[ci * c:ci * c + 1] + kvt
        st_ref[h] = st
        o = o_intra + jnp.concatenate(o_inter, axis=0)
        go = go_ref[0, :, h * GLA_DV:(h + 1) * GLA_DV].astype(F32)
        o_ref[0, :, h * GLA_DV:(h + 1) * GLA_DV] = (_rms(o, nw) * (go * jax.nn.sigmoid(go))).astype(BF16)


def _gla(proj3, lr3, w_gup, b_g, norm_w):
    b, s, _ = proj3.shape
    t = T_GLA
    assert s % t == 0 and t % GLA_CHUNK == 0
    kern = functools.partial(_gla_kernel, t=t)
    return pl.pallas_call(
        kern,
        grid=(b, s // t),
        in_specs=[
            pl.BlockSpec((1, t, GLA_KW), lambda bi, ti: (bi, ti, OFF_GQ // GLA_KW)),
            pl.BlockSpec((1, t, GLA_KW), lambda bi, ti: (bi, ti, OFF_GK // GLA_KW)),
            pl.BlockSpec((1, t, GLA_VW), lambda bi, ti: (bi, ti, OFF_GV // GLA_VW)),
            pl.BlockSpec((1, t, GLA_VW), lambda bi, ti: (bi, ti, OFF_GO // GLA_VW)),
            pl.BlockSpec((1, t, LANES), lambda bi, ti: (bi, ti, 0)),
            pl.BlockSpec((LANES, GLA_KW), lambda bi, ti: (0, 0)),
            pl.BlockSpec((1, GLA_KW), lambda bi, ti: (0, 0)),
            pl.BlockSpec((1, GLA_DV), lambda bi, ti: (0, 0)),
        ],
        out_specs=pl.BlockSpec((1, t, GLA_VW), lambda bi, ti: (bi, ti, 0)),
        out_shape=jax.ShapeDtypeStruct((b, s, GLA_VW), BF16),
        scratch_shapes=[pltpu.VMEM((GLA_HEADS, GLA_DV, GLA_KW), F32)],
        compiler_params=pltpu.CompilerParams(
            dimension_semantics=("parallel", "arbitrary"), vmem_limit_bytes=VMEM_LIMIT),
        name="gla",
    )(proj3, proj3, proj3, proj3, lr3, w_gup, b_g, norm_w)


def _outproj_kernel(ya_ref, yl_ref, yg_ref, w_ref, x_ref, nw_ref, o_ref):
    y = jnp.concatenate([ya_ref[...], yl_ref[...], yg_ref[...]], axis=1)
    mix = jnp.dot(y, w_ref[...], preferred_element_type=F32)
    o_ref[...] = x_ref[...] + _rms(mix, nw_ref[...])


def _outproj(ya, yl, yg, w_out_b, layer, x2, norm_w):
    m = x2.shape[0]
    tm = TM_OUT
    assert m % tm == 0
    return pl.pallas_call(
        _outproj_kernel,
        grid=(m // tm,),
        in_specs=[
            pl.BlockSpec((tm, DIFF_W), lambda i: (i, 0)),
            pl.BlockSpec((tm, LRU_W), lambda i: (i, 0)),
            pl.BlockSpec((tm, GLA_VW), lambda i: (i, 0)),
            pl.BlockSpec((None, D_MIX, D_MODEL), lambda i: (layer, 0, 0)),
            pl.BlockSpec((tm, D_MODEL), lambda i: (i, 0)),
            pl.BlockSpec((1, D_MODEL), lambda i: (0, 0)),
        ],
        out_specs=pl.BlockSpec((tm, D_MODEL), lambda i: (i, 0)),
        out_shape=jax.ShapeDtypeStruct((m, D_MODEL), F32),
        compiler_params=pltpu.CompilerParams(
            dimension_semantics=("parallel",), vmem_limit_bytes=VMEM_LIMIT),
        name="outproj",
    )(ya, yl, yg, w_out_b, x2, norm_w)


def _ffn_kernel(x_ref, nw1_ref, wg_ref, wu_ref, wd_ref, nw2_ref, o_ref, h_ref, acc_ref):
    f = pl.program_id(1)

    @pl.when(f == 0)
    def _():
        h_ref[...] = _rms(x_ref[...], nw1_ref[...]).astype(BF16)
        acc_ref[...] = jnp.zeros_like(acc_ref)

    h = h_ref[...]
    g = jnp.dot(h, wg_ref[...], preferred_element_type=F32)
    u = jnp.dot(h, wu_ref[...], preferred_element_type=F32)
    hid = (g * jax.nn.sigmoid(g) * u).astype(BF16)
    acc_ref[...] += jnp.dot(hid, wd_ref[...], preferred_element_type=F32)

    @pl.when(f == pl.num_programs(1) - 1)
    def _():
        o_ref[...] = x_ref[...] + _rms(acc_ref[...], nw2_ref[...])


def _ffn(x2, nw1, w_gate_b, w_up_b, w_down_b, layer, nw2):
    m = x2.shape[0]
    tm, tf = TM_FFN, TF_FFN
    assert m % tm == 0 and D_FF % tf == 0
    return pl.pallas_call(
        _ffn_kernel,
        grid=(m // tm, D_FF // tf),
        in_specs=[
            pl.BlockSpec((tm, D_MODEL), lambda i, f: (i, 0)),
            pl.BlockSpec((1, D_MODEL), lambda i, f: (0, 0)),
            pl.BlockSpec((None, D_MODEL, tf), lambda i, f: (layer, 0, f)),
            pl.BlockSpec((None, D_MODEL, tf), lambda i, f: (layer, 0, f)),
            pl.BlockSpec((None, tf, D_MODEL), lambda i, f: (layer, f, 0)),
            pl.BlockSpec((1, D_MODEL), lambda i, f: (0, 0)),
        ],
        out_specs=pl.BlockSpec((tm, D_MODEL), lambda i, f: (i, 0)),
        out_shape=jax.ShapeDtypeStruct((m, D_MODEL), F32),
        scratch_shapes=[pltpu.VMEM((tm, D_MODEL), BF16), pltpu.VMEM((tm, D_MODEL), F32)],
        compiler_params=pltpu.CompilerParams(
            dimension_semantics=("parallel", "arbitrary"), vmem_limit_bytes=VMEM_LIMIT),
        name="ffn",
    )(x2, nw1, w_gate_b, w_up_b, w_down_b, nw2)


def _rope_tables(seq):
    half = DIFF_HD // 2
    inv = ROPE_THETA ** (-jnp.arange(0, DIFF_HD, 2, dtype=F32) / DIFF_HD)
    ang = jnp.arange(seq, dtype=F32)[:, None] * inv[None, :]
    reps = LANES // half
    sign = jnp.tile(jnp.concatenate([-jnp.ones((half,), F32), jnp.ones((half,), F32)]), LANES // DIFF_HD)
    return jnp.stack([jnp.tile(jnp.cos(ang), (1, reps)), jnp.tile(jnp.sin(ang), (1, reps)) * sign[None, :]])


def _block_diag(w):
    n, c, d = w.shape
    eye = jnp.eye(n, dtype=w.dtype)
    return (eye[:, None, :, None] * w[:, :, None, :]).reshape(n * c, n * d)


@jax.jit
def _forward(x, pre_mix_norm, post_mix_norm, pre_ffn_norm, post_ffn_norm, w_in, w_out,
             lambda_q1, lambda_k1, lambda_q2, lambda_k2, diff_subln,
             conv_w, conv_b, w_rgate, b_rgate, w_igate, b_igate, lru_lambda,
             w_gla_gate_up, b_gla_gate, gla_norm, w_ffn_gate, w_ffn_up, w_ffn_down):
    b, s, dm = x.shape
    m = b * s
    rope_tab = _rope_tables(s)
    x2 = x.reshape(m, dm)
    row = lambda v: v.reshape(1, -1).astype(F32)
    w_in_b, w_out_b = w_in.astype(BF16), w_out.astype(BF16)
    w_gate_b, w_up_b, w_down_b = w_ffn_gate.astype(BF16), w_ffn_up.astype(BF16), w_ffn_down.astype(BF16)
    for l in range(DEPTH):
        lam_init = 0.8 - 0.6 * math.exp(-0.3 * l)
        w_lr = jnp.pad(w_in[l, :, N_MAIN:], ((0, 0), (0, LANES - GLA_RANK))).astype(BF16)
        proj, lr = _inproj(x2, row(pre_mix_norm[l]), w_in_b, w_lr, l)
        proj3 = proj.reshape(b, s, N_MAIN)
        lr3 = lr.reshape(b, s, LANES)

        lamv = jnp.pad(jnp.stack([lambda_q1[l], lambda_k1[l], lambda_q2[l], lambda_k2[l]]).astype(F32),
                       ((0, 0), (0, LANES - DIFF_HD)))
        y_attn = _diff_attention(proj3, rope_tab, lamv, row(diff_subln[l]), lam_init)

        w_gates = jnp.concatenate([_block_diag(w_rgate[l]), _block_diag(w_igate[l])], axis=1).astype(BF16)
        b_gates = jnp.concatenate([b_rgate[l], b_igate[l]]).reshape(1, -1).astype(F32)
        y_lru = _rglru(proj3, conv_w[l].astype(F32), row(conv_b[l]), w_gates, b_gates, row(lru_lambda[l]))

        w_gup = jnp.pad(w_gla_gate_up[l], ((0, LANES - GLA_RANK), (0, 0))).astype(BF16)
        y_gla = _gla(proj3, lr3, w_gup, row(b_gla_gate[l]), row(gla_norm[l]))

        x2 = _outproj(y_attn.reshape(m, DIFF_W), y_lru.reshape(m, LRU_W), y_gla.reshape(m, GLA_VW),
                      w_out_b, l, x2, row(post_mix_norm[l]))
        x2 = _ffn(x2, row(pre_ffn_norm[l]), w_gate_b, w_up_b, w_down_b, l, row(post_ffn_norm[l]))
    return x2.reshape(b, s, dm)


def kernel(x, pre_mix_norm, post_mix_norm, pre_ffn_norm, post_ffn_norm, w_in, w_out, lambda_q1, lambda_k1, lambda_q2, lambda_k2, diff_subln, conv_w, conv_b, w_rgate, b_rgate, w_igate, b_igate, lru_lambda, w_gla_gate_up, b_gla_gate, gla_norm, w_ffn_gate, w_ffn_up, w_ffn_down):
    return _forward(x, pre_mix_norm, post_mix_norm, pre_ffn_norm, post_ffn_norm, w_in, w_out,
                    lambda_q1, lambda_k1, lambda_q2, lambda_k2, diff_subln,
                    conv_w, conv_b, w_rgate, b_rgate, w_igate, b_igate, lru_lambda,
                    w_gla_gate_up, b_gla_gate, gla_norm, w_ffn_gate, w_ffn_up, w_ffn_down)
```

```python
import functools
import math

import jax
import jax.numpy as jnp
from jax import lax
from jax.experimental import pallas as pl
from jax.experimental.pallas import tpu as pltpu

F32 = jnp.float32
BF16 = jnp.bfloat16

D_MODEL = 2048
DEPTH = 4
DIFF_HEADS = 8
DIFF_HD = 64
DIFF_VD = 2 * DIFF_HD
DIFF_W = DIFF_HEADS * DIFF_VD
ROPE_THETA = 10000.0
LRU_W = 512
LRU_BLOCKS = 8
CONV_W = 4
LRU_C = 8.0
GLA_HEADS = 4
GLA_DK = 64
GLA_DV = 128
GLA_KW = GLA_HEADS * GLA_DK
GLA_VW = GLA_HEADS * GLA_DV
GLA_RANK = 16
GLA_NORMALIZER = 16.0
GLA_CHUNK = 64
D_MIX = DIFF_W + LRU_W + GLA_VW
N_MAIN = 2 * DIFF_W + DIFF_W + 2 * LRU_W + 2 * GLA_KW + 2 * GLA_VW
D_FF = 5632
NORM_EPS = 1e-6

LANES = 128
SUBLANES = 8
VMEM_LIMIT = 56 * 1024 * 1024

OFF_Q, OFF_K, OFF_V = 0, DIFF_W, 2 * DIFF_W
OFF_LRU_G, OFF_LRU_X = 3 * DIFF_W, 3 * DIFF_W + LRU_W
OFF_GQ = OFF_LRU_X + LRU_W
OFF_GK = OFF_GQ + GLA_KW
OFF_GV = OFF_GK + GLA_KW
OFF_GO = OFF_GV + GLA_VW

TM_PROJ = 1024
TN_PROJ = 512
TQ_ATTN = 512
ATTN_HEADS_PER_STEP = 2
Q_SCALE = DIFF_HD ** -0.5 * math.log2(math.e)
T_LRU = 1024
T_GLA = 256
TM_OUT = 512
TM_FFN = 512
TF_FFN = 512
TAIL = 16


def _rms(x, w):
    ms = jnp.mean(x * x, axis=-1, keepdims=True)
    return x * lax.rsqrt(ms + NORM_EPS) * w


def _inproj_kernel(x_ref, nw_ref, w_ref, wlr_ref, o_ref, lr_ref, h_ref):
    @pl.when(pl.program_id(1) == 0)
    def _():
        hb = _rms(x_ref[...], nw_ref[...]).astype(BF16)
        h_ref[...] = hb
        lr_ref[...] = jnp.dot(hb, wlr_ref[...], preferred_element_type=F32).astype(BF16)

    h = h_ref[...]
    half = o_ref.shape[1] // 2
    for c in range(2):
        cols = slice(c * half, (c + 1) * half)
        o_ref[:, cols] = jnp.dot(h, w_ref[:, cols], preferred_element_type=F32).astype(BF16)


def _inproj(x2, norm_w, w_in_b, w_lr, layer):
    m = x2.shape[0]
    tm, tn = TM_PROJ, TN_PROJ
    assert m % tm == 0 and N_MAIN % tn == 0
    return pl.pallas_call(
        _inproj_kernel,
        grid=(m // tm, N_MAIN // tn),
        in_specs=[
            pl.BlockSpec((tm, D_MODEL), lambda i, j: (i, 0)),
            pl.BlockSpec((1, D_MODEL), lambda i, j: (0, 0)),
            pl.BlockSpec((None, D_MODEL, tn), lambda i, j: (layer, 0, j)),
            pl.BlockSpec((D_MODEL, LANES), lambda i, j: (0, 0)),
        ],
        out_specs=[
            pl.BlockSpec((tm, tn), lambda i, j: (i, j)),
            pl.BlockSpec((tm, LANES), lambda i, j: (i, 0)),
        ],
        out_shape=[
            jax.ShapeDtypeStruct((m, N_MAIN), BF16),
            jax.ShapeDtypeStruct((m, LANES), BF16),
        ],
        scratch_shapes=[pltpu.VMEM((tm, D_MODEL), BF16)],
        compiler_params=pltpu.CompilerParams(
            dimension_semantics=("parallel", "arbitrary"), vmem_limit_bytes=VMEM_LIMIT),
        name="inproj",
    )(x2, norm_w, w_in_b, w_lr)


def _rope(a, cos, sin):
    lane = lax.broadcasted_iota(jnp.int32, a.shape, 1)
    lower_half = (lane & (DIFF_HD // 2)) == 0
    rot = jnp.where(lower_half,
                    pltpu.roll(a, LANES - DIFF_HD // 2, 1),
                    pltpu.roll(a, DIFF_HD // 2, 1))
    return a * cos + rot * sin


def _stack_maps(q):
    lane = lax.broadcasted_iota(jnp.int32, q.shape, 1)
    stacked = jnp.concatenate([jnp.where(lane < DIFF_HD, q, 0.0),
                               jnp.where(lane >= DIFF_HD, q, 0.0)], axis=0)
    return stacked.T.astype(BF16)


def _attn_kernel(q_ref, k_ref, v_ref, tab_ref, lamv_ref, sw_ref, o_ref,
                 kr_ref, vt_ref, sa_ref, sb_ref, qq_ref, qn_ref, m_ref, l_ref, acc_ref,
                 *, tq, tk, lam_init):
    nq = q_ref.shape[1] // tq
    heads = range(ATTN_HEADS_PER_STEP)

    def head_lanes(h):
        return slice(h * LANES, (h + 1) * LANES)

    def rotated_q(h, tile):
        rows = pl.ds(pl.multiple_of(tile * tq, tq), tq)
        q = _rope(q_ref[0, rows, head_lanes(h)].astype(F32), tab_ref[0, rows, :], tab_ref[1, rows, :])
        return _stack_maps(q * Q_SCALE)

    def scores(h, qq, blk, s_ref):
        kb = kr_ref[h, pl.ds(pl.multiple_of(blk * tk, tk), tk), :]
        s_ref[h] = jnp.dot(kb, qq, preferred_element_type=F32)

    def process(h, s_ref, blk, mask):
        s = s_ref[h]
        if mask is not None:
            s = jnp.where(mask, s, -jnp.inf)
        m = m_ref[h]
        m_new = jnp.maximum(m, jnp.max(s, axis=0, keepdims=True))
        alpha = jnp.exp2(m - m_new)
        m_ref[h] = m_new
        p = jnp.exp2(s - m_new).astype(BF16)
        pv = jnp.dot(vt_ref[h, blk], p, preferred_element_type=F32)
        l_ref[h] = alpha * l_ref[h] + pv[DIFF_VD:DIFF_VD + 1]
        acc_ref[h] = alpha * acc_ref[h] + pv[:DIFF_VD]

    ones = jnp.ones((16, tk), BF16)
    for h in heads:
        kr_ref[h] = _rope(k_ref[0, :, head_lanes(h)].astype(F32), tab_ref[0], tab_ref[1]).astype(BF16)
        for j in range(vt_ref.shape[1]):
            vt = v_ref[0, j * tk:(j + 1) * tk, head_lanes(h)].astype(F32).T.astype(BF16)
            vt_ref[h, j] = jnp.concatenate([vt, ones], axis=0)
        qn_ref[h] = rotated_q(h, 0)
        scores(h, qn_ref[h], 0, sa_ref)

    lv = lamv_ref[...]
    lam = (jnp.exp(jnp.sum(lv[0:1] * lv[1:2], axis=-1, keepdims=True))
           - jnp.exp(jnp.sum(lv[2:3] * lv[3:4], axis=-1, keepdims=True)) + lam_init)

    def pair(p):
        for h in heads:
            scores(h, qq_ref[h], 2 * p + 1, sb_ref)
        for h in heads:
            process(h, sa_ref, 2 * p, None)
        for h in heads:
            scores(h, qq_ref[h], 2 * p + 2, sa_ref)
        for h in heads:
            process(h, sb_ref, 2 * p + 1, None)

    def two_pairs(j, first):
        pair(first + 2 * j)
        pair(first + 2 * j + 1)
        return first

    def tile(qi, carry):
        qq_ref[...] = qn_ref[...]
        m_ref[...] = jnp.full_like(m_ref, -jnp.inf)
        l_ref[...] = jnp.zeros_like(l_ref)
        acc_ref[...] = jnp.zeros_like(acc_ref)
        odd = qi & 1

        @pl.when(odd == 1)
        def _():
            pair(0)

        lax.fori_loop(0, qi // 2, two_pairs, odd)

        key = lax.broadcasted_iota(jnp.int32, (tk, 2 * tq), 0)
        qry = lax.broadcasted_iota(jnp.int32, (tk, 2 * tq), 1) & (tq - 1)
        d0 = 2 * qi
        for h in heads:
            scores(h, qq_ref[h], d0 + 1, sb_ref)
        for h in heads:
            process(h, sa_ref, d0, key <= qry)
        for h in heads:
            qn_ref[h] = rotated_q(h, jnp.minimum(qi + 1, nq - 1))
            scores(h, qn_ref[h], 0, sa_ref)
        for h in heads:
            process(h, sb_ref, d0 + 1, key + tk <= qry)

        for h in heads:
            o = acc_ref[h] / l_ref[h]
            d = o[:, :tq] - lam * o[:, tq:]
            ms = jnp.mean(d * d, axis=0, keepdims=True)
            y = (d * lax.rsqrt(ms + NORM_EPS)).T * (sw_ref[...] * (1.0 - lam_init))
            o_ref[0, pl.ds(pl.multiple_of(qi * tq, tq), tq), head_lanes(h)] = y.astype(BF16)
        return carry

    lax.fori_loop(0, nq, tile, 0)


def _diff_attention(proj3, rope_tab, lamv, subln_w, lam_init):
    b, s, _ = proj3.shape
    tq = TQ_ATTN
    tk = tq // 2
    hp = ATTN_HEADS_PER_STEP
    w = hp * LANES
    assert s % tq == 0 and tq & (tq - 1) == 0 and DIFF_HEADS % hp == 0
    kb0, vb0 = OFF_K // w, OFF_V // w
    kern = functools.partial(_attn_kernel, tq=tq, tk=tk, lam_init=lam_init)
    return pl.pallas_call(
        kern,
        grid=(b, DIFF_HEADS // hp),
        in_specs=[
            pl.BlockSpec((1, s, w), lambda bi, h: (bi, 0, h)),
            pl.BlockSpec((1, s, w), lambda bi, h: (bi, 0, kb0 + h)),
            pl.BlockSpec((1, s, w), lambda bi, h: (bi, 0, vb0 + h)),
            pl.BlockSpec((2, s, LANES), lambda bi, h: (0, 0, 0)),
            pl.BlockSpec((4, LANES), lambda bi, h: (0, 0)),
            pl.BlockSpec((1, DIFF_VD), lambda bi, h: (0, 0)),
        ],
        out_specs=pl.BlockSpec((1, s, w), lambda bi, h: (bi, 0, h)),
        out_shape=jax.ShapeDtypeStruct((b, s, DIFF_W), BF16),
        scratch_shapes=[
            pltpu.VMEM((hp, s, LANES), BF16), pltpu.VMEM((hp, s // tk, DIFF_VD + 16, tk), BF16),
            pltpu.VMEM((hp, tk, 2 * tq), F32), pltpu.VMEM((hp, tk, 2 * tq), F32),
            pltpu.VMEM((hp, LANES, 2 * tq), BF16), pltpu.VMEM((hp, LANES, 2 * tq), BF16),
            pltpu.VMEM((hp, 1, 2 * tq), F32), pltpu.VMEM((hp, 1, 2 * tq), F32),
            pltpu.VMEM((hp, DIFF_VD, 2 * tq), F32),
        ],
        compiler_params=pltpu.CompilerParams(
            dimension_semantics=("parallel", "parallel"), vmem_limit_bytes=VMEM_LIMIT),
        name="diff_attn",
    )(proj3, proj3, proj3, rope_tab, lamv, subln_w)


def _lru_kernel(g_ref, x_ref, tail_ref, cw_ref, cb_ref, wg_ref, bg_ref, lam_ref, o_ref, h_ref, *, t):
    ti = pl.program_id(1)

    @pl.when(ti == 0)
    def _():
        h_ref[...] = jnp.zeros_like(h_ref)

    xr = x_ref[0].astype(F32)
    tail = jnp.where(ti > 0, tail_ref[0].astype(F32), 0.0)
    ext = jnp.concatenate([tail, xr], axis=0)
    cw = cw_ref[...]
    xc = cb_ref[...] + cw[CONV_W - 1:CONV_W] * xr
    for d in range(1, CONV_W):
        xc = xc + cw[CONV_W - 1 - d:CONV_W - d] * pltpu.roll(ext, d, 0)[TAIL:]

    gates = jnp.dot(xc.astype(BF16), wg_ref[...], preferred_element_type=F32) + bg_ref[...]
    r = jax.nn.sigmoid(gates[:, :LRU_W])
    i = jax.nn.sigmoid(gates[:, LRU_W:])
    lam = lam_ref[...]
    softplus_neg_lam = jnp.maximum(-lam, 0.0) + jnp.log1p(jnp.exp(-jnp.abs(lam)))
    log_a = -LRU_C * r * softplus_neg_lam
    a = jnp.exp(log_a)
    u = jnp.sqrt(-jnp.tanh(log_a) * (a * a + 1.0)) * (i * xc)

    in_group = lax.broadcasted_iota(jnp.int32, (t // SUBLANES, SUBLANES, LRU_W), 1)
    u = u.reshape(t // SUBLANES, SUBLANES, LRU_W)
    a = a.reshape(t // SUBLANES, SUBLANES, LRU_W)
    d = 1
    while d < SUBLANES:
        keep = in_group >= d
        u = a * jnp.where(keep, pltpu.roll(u, d, 1), 0.0) + u
        a = a * jnp.where(keep, pltpu.roll(a, d, 1), 1.0)
        d *= 2
    u = u.reshape(t, LRU_W)
    a = a.reshape(t, LRU_W)
    carry = h_ref[...]
    groups = []
    for g in range(t // SUBLANES):
        rows = slice(g * SUBLANES, (g + 1) * SUBLANES)
        hg = u[rows] + a[rows] * carry
        groups.append(hg)
        carry = hg[SUBLANES - 1:SUBLANES]
    h = jnp.concatenate(groups, axis=0)
    h_ref[...] = carry

    xg = g_ref[0].astype(F32)
    gelu = 0.5 * xg * (1.0 + jnp.tanh(math.sqrt(2.0 / math.pi) * (xg + 0.044715 * (xg * xg * xg))))
    o_ref[0] = (h * gelu).astype(BF16)


def _rglru(proj3, conv_w, conv_b, w_gates, b_gates, lru_lambda):
    b, s, _ = proj3.shape
    t = T_LRU
    assert s % t == 0 and t % TAIL == 0
    gb, xb = OFF_LRU_G // LRU_W, OFF_LRU_X // LRU_W
    kern = functools.partial(_lru_kernel, t=t)
    return pl.pallas_call(
        kern,
        grid=(b, s // t),
        in_specs=[
            pl.BlockSpec((1, t, LRU_W), lambda bi, ti: (bi, ti, gb)),
            pl.BlockSpec((1, t, LRU_W), lambda bi, ti: (bi, ti, xb)),
            pl.BlockSpec((1, TAIL, LRU_W), lambda bi, ti: (bi, jnp.maximum(ti * (t // TAIL) - 1, 0), xb)),
            pl.BlockSpec((CONV_W, LRU_W), lambda bi, ti: (0, 0)),
            pl.BlockSpec((1, LRU_W), lambda bi, ti: (0, 0)),
            pl.BlockSpec((LRU_W, 2 * LRU_W), lambda bi, ti: (0, 0)),
            pl.BlockSpec((1, 2 * LRU_W), lambda bi, ti: (0, 0)),
            pl.BlockSpec((1, LRU_W), lambda bi, ti: (0, 0)),
        ],
        out_specs=pl.BlockSpec((1, t, LRU_W), lambda bi, ti: (bi, ti, 0)),
        out_shape=jax.ShapeDtypeStruct((b, s, LRU_W), BF16),
        scratch_shapes=[pltpu.VMEM((1, LRU_W), F32)],
        compiler_params=pltpu.CompilerParams(
            dimension_semantics=("parallel", "arbitrary"), vmem_limit_bytes=VMEM_LIMIT),
        name="rglru",
    )(proj3, proj3, proj3, conv_w, conv_b, w_gates, b_gates, lru_lambda)


def _gla_kernel(q_ref, k_ref, v_ref, go_ref, lr_ref, wg_ref, bg_ref, nw_ref, o_ref, st_ref, *, t):
    ti = pl.program_id(1)

    @pl.when(ti == 0)
    def _():
        st_ref[...] = jnp.zeros_like(st_ref)

    c = GLA_CHUNK
    n = t // c
    z = jnp.dot(lr_ref[0], wg_ref[...], preferred_element_type=F32) + bg_ref[...]
    gk = (jnp.minimum(z, 0.0) - jnp.log1p(jnp.exp(-jnp.abs(z)))) * (1.0 / GLA_NORMALIZER)

    row = lax.broadcasted_iota(jnp.int32, (t, GLA_KW), 0)
    rc = row & (c - 1)
    bc = gk
    d = 1
    while d < c:
        bc = bc + jnp.where(rc >= d, pltpu.roll(bc, d, 0), 0.0)
        d *= 2
    bl3 = jnp.broadcast_to(bc.reshape(n, c, GLA_KW)[:, c - 1:c, :], (n, c, GLA_KW))
    bl = bl3.reshape(t, GLA_KW)

    q = q_ref[0].astype(F32) * (GLA_DK ** -0.5)
    k = k_ref[0].astype(F32)
    qe = q * jnp.exp(bc)
    ke = k * jnp.exp(-bc)
    kd = k * jnp.exp(bl - bc)
    dec = jnp.exp(bl)

    lane = lax.broadcasted_iota(jnp.int32, (t, GLA_KW), 1)
    col = lax.broadcasted_iota(jnp.int32, (t, t), 1)
    rowt = lax.broadcasted_iota(jnp.int32, (t, t), 0)
    causal = (col <= rowt) & ((col & -c) == (rowt & -c))
    qe_b = qe.astype(BF16)
    nw = nw_ref[...]
    for h in range(GLA_HEADS):
        in_head = (lane >= h * GLA_DK) & (lane < (h + 1) * GLA_DK)
        ke_h = jnp.where(in_head, ke, 0.0).astype(BF16)
        kd_h = jnp.where(in_head, kd, 0.0).astype(BF16)
        v_h = v_ref[0, :, h * GLA_DV:(h + 1) * GLA_DV]
        att = lax.dot_general(qe_b, ke_h, (((1,), (1,)), ((), ())), preferred_element_type=F32)
        att = jnp.where(causal, att, 0.0)
        o_intra = jnp.dot(att.astype(BF16), v_h, preferred_element_type=F32)
        st = st_ref[h]
        o_inter = []
        for ci in range(n):
            sl = slice(ci * c, (ci + 1) * c)
            o_inter.append(lax.dot_general(qe_b[sl], st.astype(BF16), (((1,), (1,)), ((), ())),
                                           preferred_element_type=F32))
            kvt = lax.dot_general(v_h[sl], kd_h[sl], (((0,), (0,)), ((), ())),
                                  preferred_element_type=F32)
            st = st * dec[ci * c:ci * c + 1] + kvt
        st_ref[h] = st
        o = o_intra + jnp.concatenate(o_inter, axis=0)
        go = go_ref[0, :, h * GLA_DV:(h + 1) * GLA_DV].astype(F32)
        o_ref[0, :, h * GLA_DV:(h + 1) * GLA_DV] = (_rms(o, nw) * (go * jax.nn.sigmoid(go))).astype(BF16)


def _gla(proj3, lr3, w_gup, b_g, norm_w):
    b, s, _ = proj3.shape
    t = T_GLA
    assert s % t == 0 and t % GLA_CHUNK == 0
    kern = functools.partial(_gla_kernel, t=t)
    return pl.pallas_call(
        kern,
        grid=(b, s // t),
        in_specs=[
            pl.BlockSpec((1, t, GLA_KW), lambda bi, ti: (bi, ti, OFF_GQ // GLA_KW)),
            pl.BlockSpec((1, t, GLA_KW), lambda bi, ti: (bi, ti, OFF_GK // GLA_KW)),
            pl.BlockSpec((1, t, GLA_VW), lambda bi, ti: (bi, ti, OFF_GV // GLA_VW)),
            pl.BlockSpec((1, t, GLA_VW), lambda bi, ti: (bi, ti, OFF_GO // GLA_VW)),
            pl.BlockSpec((1, t, LANES), lambda bi, ti: (bi, ti, 0)),
            pl.BlockSpec((LANES, GLA_KW), lambda bi, ti: (0, 0)),
            pl.BlockSpec((1, GLA_KW), lambda bi, ti: (0, 0)),
            pl.BlockSpec((1, GLA_DV), lambda bi, ti: (0, 0)),
        ],
        out_specs=pl.BlockSpec((1, t, GLA_VW), lambda bi, ti: (bi, ti, 0)),
        out_shape=jax.ShapeDtypeStruct((b, s, GLA_VW), BF16),
        scratch_shapes=[pltpu.VMEM((GLA_HEADS, GLA_DV, GLA_KW), F32)],
        compiler_params=pltpu.CompilerParams(
            dimension_semantics=("parallel", "arbitrary"), vmem_limit_bytes=VMEM_LIMIT),
        name="gla",
    )(proj3, proj3, proj3, proj3, lr3, w_gup, b_g, norm_w)


def _outproj_kernel(ya_ref, yl_ref, yg_ref, w_ref, x_ref, nw_ref, o_ref):
    y = jnp.concatenate([ya_ref[...], yl_ref[...], yg_ref[...]], axis=1)
    mix = jnp.dot(y, w_ref[...], preferred_element_type=F32)
    o_ref[...] = x_ref[...] + _rms(mix, nw_ref[...])


def _outproj(ya, yl, yg, w_out_b, layer, x2, norm_w):
    m = x2.shape[0]
    tm = TM_OUT
    assert m % tm == 0
    return pl.pallas_call(
        _outproj_kernel,
        grid=(m // tm,),
        in_specs=[
            pl.BlockSpec((tm, DIFF_W), lambda i: (i, 0)),
            pl.BlockSpec((tm, LRU_W), lambda i: (i, 0)),
            pl.BlockSpec((tm, GLA_VW), lambda i: (i, 0)),
            pl.BlockSpec((None, D_MIX, D_MODEL), lambda i: (layer, 0, 0)),
            pl.BlockSpec((tm, D_MODEL), lambda i: (i, 0)),
            pl.BlockSpec((1, D_MODEL), lambda i: (0, 0)),
        ],
        out_specs=pl.BlockSpec((tm, D_MODEL), lambda i: (i, 0)),
        out_shape=jax.ShapeDtypeStruct((m, D_MODEL), F32),
        compiler_params=pltpu.CompilerParams(
            dimension_semantics=("parallel",), vmem_limit_bytes=VMEM_LIMIT),
        name="outproj",
    )(ya, yl, yg, w_out_b, x2, norm_w)


def _ffn_kernel(x_ref, nw1_ref, wg_ref, wu_ref, wd_ref, nw2_ref, o_ref, h_ref, acc_ref):
    f = pl.program_id(1)

    @pl.when(f == 0)
    def _():
        h_ref[...] = _rms(x_ref[...], nw1_ref[...]).astype(BF16)
        acc_ref[...] = jnp.zeros_like(acc_ref)

    h = h_ref[...]
    g = jnp.dot(h, wg_ref[...], preferred_element_type=F32)
    u = jnp.dot(h, wu_ref[...], preferred_element_type=F32)
    hid = (g * jax.nn.sigmoid(g) * u).astype(BF16)
    acc_ref[...] += jnp.dot(hid, wd_ref[...], preferred_element_type=F32)

    @pl.when(f == pl.num_programs(1) - 1)
    def _():
        o_ref[...] = x_ref[...] + _rms(acc_ref[...], nw2_ref[...])


def _ffn(x2, nw1, w_gate_b, w_up_b, w_down_b, layer, nw2):
    m = x2.shape[0]
    tm, tf = TM_FFN, TF_FFN
    assert m % tm == 0 and D_FF % tf == 0
    return pl.pallas_call(
        _ffn_kernel,
        grid=(m // tm, D_FF // tf),
        in_specs=[
            pl.BlockSpec((tm, D_MODEL), lambda i, f: (i, 0)),
            pl.BlockSpec((1, D_MODEL), lambda i, f: (0, 0)),
            pl.BlockSpec((None, D_MODEL, tf), lambda i, f: (layer, 0, f)),
            pl.BlockSpec((None, D_MODEL, tf), lambda i, f: (layer, 0, f)),
            pl.BlockSpec((None, tf, D_MODEL), lambda i, f: (layer, f, 0)),
            pl.BlockSpec((1, D_MODEL), lambda i, f: (0, 0)),
        ],
        out_specs=pl.BlockSpec((tm, D_MODEL), lambda i, f: (i, 0)),
        out_shape=jax.ShapeDtypeStruct((m, D_MODEL), F32),
        scratch_shapes=[pltpu.VMEM((tm, D_MODEL), BF16), pltpu.VMEM((tm, D_MODEL), F32)],
        compiler_params=pltpu.CompilerParams(
            dimension_semantics=("parallel", "arbitrary"), vmem_limit_bytes=VMEM_LIMIT),
        name="ffn",
    )(x2, nw1, w_gate_b, w_up_b, w_down_b, nw2)


def _rope_tables(seq):
    half = DIFF_HD // 2
    inv = ROPE_THETA ** (-jnp.arange(0, DIFF_HD, 2, dtype=F32) / DIFF_HD)
    ang = jnp.arange(seq, dtype=F32)[:, None] * inv[None, :]
    reps = LANES // half
    sign = jnp.tile(jnp.concatenate([-jnp.ones((half,), F32), jnp.ones((half,), F32)]), LANES // DIFF_HD)
    return jnp.stack([jnp.tile(jnp.cos(ang), (1, reps)), jnp.tile(jnp.sin(ang), (1, reps)) * sign[None, :]])


def _block_diag(w):
    n, c, d = w.shape
    eye = jnp.eye(n, dtype=w.dtype)
    return (eye[:, None, :, None] * w[:, :, None, :]).reshape(n * c, n * d)


@jax.jit
def _forward(x, pre_mix_norm, post_mix_norm, pre_ffn_norm, post_ffn_norm, w_in, w_out,
             lambda_q1, lambda_k1, lambda_q2, lambda_k2, diff_subln,
             conv_w, conv_b, w_rgate, b_rgate, w_igate, b_igate, lru_lambda,
             w_gla_gate_up, b_gla_gate, gla_norm, w_ffn_gate, w_ffn_up, w_ffn_down):
    b, s, dm = x.shape
    m = b * s
    rope_tab = _rope_tables(s)
    x2 = x.reshape(m, dm)
    row = lambda v: v.reshape(1, -1).astype(F32)
    w_in_b, w_out_b = w_in.astype(BF16), w_out.astype(BF16)
    w_gate_b, w_up_b, w_down_b = w_ffn_gate.astype(BF16), w_ffn_up.astype(BF16), w_ffn_down.astype(BF16)
    for l in range(DEPTH):
        lam_init = 0.8 - 0.6 * math.exp(-0.3 * l)
        w_lr = jnp.pad(w_in[l, :, N_MAIN:], ((0, 0), (0, LANES - GLA_RANK))).astype(BF16)
        proj, lr = _inproj(x2, row(pre_mix_norm[l]), w_in_b, w_lr, l)
        proj3 = proj.reshape(b, s, N_MAIN)
        lr3 = lr.reshape(b, s, LANES)

        lamv = jnp.pad(jnp.stack([lambda_q1[l], lambda_k1[l], lambda_q2[l], lambda_k2[l]]).astype(F32),
                       ((0, 0), (0, LANES - DIFF_HD)))
        y_attn = _diff_attention(proj3, rope_tab, lamv, row(diff_subln[l]), lam_init)

        w_gates = jnp.concatenate([_block_diag(w_rgate[l]), _block_diag(w_igate[l])], axis=1).astype(BF16)
        b_gates = jnp.concatenate([b_rgate[l], b_igate[l]]).reshape(1, -1).astype(F32)
        y_lru = _rglru(proj3, conv_w[l].astype(F32), row(conv_b[l]), w_gates, b_gates, row(lru_lambda[l]))

        w_gup = jnp.pad(w_gla_gate_up[l], ((0, LANES - GLA_RANK), (0, 0))).astype(BF16)
        y_gla = _gla(proj3, lr3, w_gup, row(b_gla_gate[l]), row(gla_norm[l]))

        x2 = _outproj(y_attn.reshape(m, DIFF_W), y_lru.reshape(m, LRU_W), y_gla.reshape(m, GLA_VW),
                      w_out_b, l, x2, row(post_mix_norm[l]))
        x2 = _ffn(x2, row(pre_ffn_norm[l]), w_gate_b, w_up_b, w_down_b, l, row(post_ffn_norm[l]))
    return x2.reshape(b, s, dm)


def kernel(x, pre_mix_norm, post_mix_norm, pre_ffn_norm, post_ffn_norm, w_in, w_out, lambda_q1, lambda_k1, lambda_q2, lambda_k2, diff_subln, conv_w, conv_b, w_rgate, b_rgate, w_igate, b_igate, lru_lambda, w_gla_gate_up, b_gla_gate, gla_norm, w_ffn_gate, w_ffn_up, w_ffn_down):
    return _forward(x, pre_mix_norm, post_mix_norm, pre_ffn_norm, post_ffn_norm, w_in, w_out,
                    lambda_q1, lambda_k1, lambda_q2, lambda_k2, diff_subln,
                    conv_w, conv_b, w_rgate, b_rgate, w_igate, b_igate, lru_lambda,
                    w_gla_gate_up, b_gla_gate, gla_norm, w_ffn_gate, w_ffn_up, w_ffn_down)
```

```python
import functools
import math

import jax
import jax.numpy as jnp
from jax import lax
from jax.experimental import pallas as pl
from jax.experimental.pallas import tpu as pltpu

F32 = jnp.float32
BF16 = jnp.bfloat16

D_MODEL = 2048
DEPTH = 4
DIFF_HEADS = 8
DIFF_HD = 64
DIFF_VD = 2 * DIFF_HD
DIFF_W = DIFF_HEADS * DIFF_VD
ROPE_THETA = 10000.0
LRU_W = 512
LRU_BLOCKS = 8
CONV_W = 4
LRU_C = 8.0
GLA_HEADS = 4
GLA_DK = 64
GLA_DV = 128
GLA_KW = GLA_HEADS * GLA_DK
GLA_VW = GLA_HEADS * GLA_DV
GLA_RANK = 16
GLA_NORMALIZER = 16.0
GLA_CHUNK = 64
D_MIX = DIFF_W + LRU_W + GLA_VW
N_MAIN = 2 * DIFF_W + DIFF_W + 2 * LRU_W + 2 * GLA_KW + 2 * GLA_VW
D_FF = 5632
NORM_EPS = 1e-6

LANES = 128
SUBLANES = 8
VMEM_LIMIT = 56 * 1024 * 1024

OFF_Q, OFF_K, OFF_V = 0, DIFF_W, 2 * DIFF_W
OFF_LRU_G, OFF_LRU_X = 3 * DIFF_W, 3 * DIFF_W + LRU_W
OFF_GQ = OFF_LRU_X + LRU_W
OFF_GK = OFF_GQ + GLA_KW
OFF_GV = OFF_GK + GLA_KW
OFF_GO = OFF_GV + GLA_VW

TM_PROJ = 1024
TN_PROJ = 512
TQ_ATTN = 512
ATTN_HEADS_PER_STEP = 2
Q_SCALE = DIFF_HD ** -0.5 * math.log2(math.e)
T_LRU = 1024
T_GLA = 256
GLA_BATCH_PER_STEP = 4
TM_OUT = 512
TM_FFN = 512
TF_FFN = 512
TAIL = 16


def _rms(x, w):
    ms = jnp.mean(x * x, axis=-1, keepdims=True)
    return x * lax.rsqrt(ms + NORM_EPS) * w


def _inproj_kernel(x_ref, nw_ref, w_ref, wlr_ref, o_ref, lr_ref, h_ref):
    @pl.when(pl.program_id(1) == 0)
    def _():
        hb = _rms(x_ref[...], nw_ref[...]).astype(BF16)
        h_ref[...] = hb
        lr_ref[...] = jnp.dot(hb, wlr_ref[...], preferred_element_type=F32).astype(BF16)

    h = h_ref[...]
    half = o_ref.shape[1] // 2
    for c in range(2):
        cols = slice(c * half, (c + 1) * half)
        o_ref[:, cols] = jnp.dot(h, w_ref[:, cols], preferred_element_type=F32).astype(BF16)


def _inproj(x2, norm_w, w_in_b, w_lr, layer):
    m = x2.shape[0]
    tm, tn = TM_PROJ, TN_PROJ
    assert m % tm == 0 and N_MAIN % tn == 0
    return pl.pallas_call(
        _inproj_kernel,
        grid=(m // tm, N_MAIN // tn),
        in_specs=[
            pl.BlockSpec((tm, D_MODEL), lambda i, j: (i, 0)),
            pl.BlockSpec((1, D_MODEL), lambda i, j: (0, 0)),
            pl.BlockSpec((None, D_MODEL, tn), lambda i, j: (layer, 0, j)),
            pl.BlockSpec((D_MODEL, LANES), lambda i, j: (0, 0)),
        ],
        out_specs=[
            pl.BlockSpec((tm, tn), lambda i, j: (i, j)),
            pl.BlockSpec((tm, LANES), lambda i, j: (i, 0)),
        ],
        out_shape=[
            jax.ShapeDtypeStruct((m, N_MAIN), BF16),
            jax.ShapeDtypeStruct((m, LANES), BF16),
        ],
        scratch_shapes=[pltpu.VMEM((tm, D_MODEL), BF16)],
        compiler_params=pltpu.CompilerParams(
            dimension_semantics=("parallel", "arbitrary"), vmem_limit_bytes=VMEM_LIMIT),
        name="inproj",
    )(x2, norm_w, w_in_b, w_lr)


def _rope(a, cos, sin):
    lane = lax.broadcasted_iota(jnp.int32, a.shape, 1)
    lower_half = (lane & (DIFF_HD // 2)) == 0
    rot = jnp.where(lower_half,
                    pltpu.roll(a, LANES - DIFF_HD // 2, 1),
                    pltpu.roll(a, DIFF_HD // 2, 1))
    return a * cos + rot * sin


def _stack_maps(q):
    lane = lax.broadcasted_iota(jnp.int32, q.shape, 1)
    stacked = jnp.concatenate([jnp.where(lane < DIFF_HD, q, 0.0),
                               jnp.where(lane >= DIFF_HD, q, 0.0)], axis=0)
    return stacked.T.astype(BF16)


def _attn_kernel(q_ref, k_ref, v_ref, tab_ref, lamv_ref, sw_ref, o_ref,
                 kr_ref, vt_ref, sa_ref, sb_ref, qq_ref, qn_ref, m_ref, l_ref, acc_ref,
                 *, tq, tk, lam_init):
    nq = q_ref.shape[1] // tq
    heads = range(ATTN_HEADS_PER_STEP)

    def head_lanes(h):
        return slice(h * LANES, (h + 1) * LANES)

    def rotated_q(h, tile):
        rows = pl.ds(pl.multiple_of(tile * tq, tq), tq)
        q = _rope(q_ref[0, rows, head_lanes(h)].astype(F32), tab_ref[0, rows, :], tab_ref[1, rows, :])
        return _stack_maps(q * Q_SCALE)

    def scores(h, qq, blk, s_ref):
        kb = kr_ref[h, pl.ds(pl.multiple_of(blk * tk, tk), tk), :]
        s_ref[h] = jnp.dot(kb, qq, preferred_element_type=F32)

    def process(h, s_ref, blk, mask):
        s = s_ref[h]
        if mask is not None:
            s = jnp.where(mask, s, -jnp.inf)
        m = m_ref[h]
        m_new = jnp.maximum(m, jnp.max(s, axis=0, keepdims=True))
        alpha = jnp.exp2(m - m_new)
        m_ref[h] = m_new
        p = jnp.exp2(s - m_new).astype(BF16)
        pv = jnp.dot(vt_ref[h, blk], p, preferred_element_type=F32)
        l_ref[h] = alpha * l_ref[h] + pv[DIFF_VD:DIFF_VD + 1]
        acc_ref[h] = alpha * acc_ref[h] + pv[:DIFF_VD]

    ones = jnp.ones((16, tk), BF16)
    for h in heads:
        kr_ref[h] = _rope(k_ref[0, :, head_lanes(h)].astype(F32), tab_ref[0], tab_ref[1]).astype(BF16)
        for j in range(vt_ref.shape[1]):
            vt = v_ref[0, j * tk:(j + 1) * tk, head_lanes(h)].astype(F32).T.astype(BF16)
            vt_ref[h, j] = jnp.concatenate([vt, ones], axis=0)
        qn_ref[h] = rotated_q(h, 0)
        scores(h, qn_ref[h], 0, sa_ref)

    lv = lamv_ref[...]
    lam = (jnp.exp(jnp.sum(lv[0:1] * lv[1:2], axis=-1, keepdims=True))
           - jnp.exp(jnp.sum(lv[2:3] * lv[3:4], axis=-1, keepdims=True)) + lam_init)

    def pair(p):
        for h in heads:
            scores(h, qq_ref[h], 2 * p + 1, sb_ref)
        for h in heads:
            process(h, sa_ref, 2 * p, None)
        for h in heads:
            scores(h, qq_ref[h], 2 * p + 2, sa_ref)
        for h in heads:
            process(h, sb_ref, 2 * p + 1, None)

    def two_pairs(j, first):
        pair(first + 2 * j)
        pair(first + 2 * j + 1)
        return first

    def tile(qi, carry):
        qq_ref[...] = qn_ref[...]
        m_ref[...] = jnp.full_like(m_ref, -jnp.inf)
        l_ref[...] = jnp.zeros_like(l_ref)
        acc_ref[...] = jnp.zeros_like(acc_ref)
        odd = qi & 1

        @pl.when(odd == 1)
        def _():
            pair(0)

        lax.fori_loop(0, qi // 2, two_pairs, odd)

        key = lax.broadcasted_iota(jnp.int32, (tk, 2 * tq), 0)
        qry = lax.broadcasted_iota(jnp.int32, (tk, 2 * tq), 1) & (tq - 1)
        d0 = 2 * qi
        for h in heads:
            scores(h, qq_ref[h], d0 + 1, sb_ref)
        for h in heads:
            process(h, sa_ref, d0, key <= qry)
        for h in heads:
            qn_ref[h] = rotated_q(h, jnp.minimum(qi + 1, nq - 1))
            scores(h, qn_ref[h], 0, sa_ref)
        for h in heads:
            process(h, sb_ref, d0 + 1, key + tk <= qry)

        for h in heads:
            o = acc_ref[h] / l_ref[h]
            d = o[:, :tq] - lam * o[:, tq:]
            ms = jnp.mean(d * d, axis=0, keepdims=True)
            y = (d * lax.rsqrt(ms + NORM_EPS)).T * (sw_ref[...] * (1.0 - lam_init))
            o_ref[0, pl.ds(pl.multiple_of(qi * tq, tq), tq), head_lanes(h)] = y.astype(BF16)
        return carry

    lax.fori_loop(0, nq, tile, 0)


def _diff_attention(proj3, rope_tab, lamv, subln_w, lam_init):
    b, s, _ = proj3.shape
    tq = TQ_ATTN
    tk = tq // 2
    hp = ATTN_HEADS_PER_STEP
    w = hp * LANES
    assert s % tq == 0 and tq & (tq - 1) == 0 and DIFF_HEADS % hp == 0
    kb0, vb0 = OFF_K // w, OFF_V // w
    kern = functools.partial(_attn_kernel, tq=tq, tk=tk, lam_init=lam_init)
    return pl.pallas_call(
        kern,
        grid=(b, DIFF_HEADS // hp),
        in_specs=[
            pl.BlockSpec((1, s, w), lambda bi, h: (bi, 0, h)),
            pl.BlockSpec((1, s, w), lambda bi, h: (bi, 0, kb0 + h)),
            pl.BlockSpec((1, s, w), lambda bi, h: (bi, 0, vb0 + h)),
            pl.BlockSpec((2, s, LANES), lambda bi, h: (0, 0, 0)),
            pl.BlockSpec((4, LANES), lambda bi, h: (0, 0)),
            pl.BlockSpec((1, DIFF_VD), lambda bi, h: (0, 0)),
        ],
        out_specs=pl.BlockSpec((1, s, w), lambda bi, h: (bi, 0, h)),
        out_shape=jax.ShapeDtypeStruct((b, s, DIFF_W), BF16),
        scratch_shapes=[
            pltpu.VMEM((hp, s, LANES), BF16), pltpu.VMEM((hp, s // tk, DIFF_VD + 16, tk), BF16),
            pltpu.VMEM((hp, tk, 2 * tq), F32), pltpu.VMEM((hp, tk, 2 * tq), F32),
            pltpu.VMEM((hp, LANES, 2 * tq), BF16), pltpu.VMEM((hp, LANES, 2 * tq), BF16),
            pltpu.VMEM((hp, 1, 2 * tq), F32), pltpu.VMEM((hp, 1, 2 * tq), F32),
            pltpu.VMEM((hp, DIFF_VD, 2 * tq), F32),
        ],
        compiler_params=pltpu.CompilerParams(
            dimension_semantics=("parallel", "parallel"), vmem_limit_bytes=VMEM_LIMIT),
        name="diff_attn",
    )(proj3, proj3, proj3, rope_tab, lamv, subln_w)


def _lru_kernel(g_ref, x_ref, tail_ref, cw_ref, cb_ref, wg_ref, bg_ref, lam_ref, o_ref, h_ref, *, t):
    ti = pl.program_id(1)

    @pl.when(ti == 0)
    def _():
        h_ref[...] = jnp.zeros_like(h_ref)

    xr = x_ref[0].astype(F32)
    tail = jnp.where(ti > 0, tail_ref[0].astype(F32), 0.0)
    ext = jnp.concatenate([tail, xr], axis=0)
    cw = cw_ref[...]
    xc = cb_ref[...] + cw[CONV_W - 1:CONV_W] * xr
    for d in range(1, CONV_W):
        xc = xc + cw[CONV_W - 1 - d:CONV_W - d] * pltpu.roll(ext, d, 0)[TAIL:]

    gates = jnp.dot(xc.astype(BF16), wg_ref[...], preferred_element_type=F32) + bg_ref[...]
    r = jax.nn.sigmoid(gates[:, :LRU_W])
    i = jax.nn.sigmoid(gates[:, LRU_W:])
    lam = lam_ref[...]
    softplus_neg_lam = jnp.maximum(-lam, 0.0) + jnp.log1p(jnp.exp(-jnp.abs(lam)))
    log_a = -LRU_C * r * softplus_neg_lam
    a = jnp.exp(log_a)
    u = jnp.sqrt(-jnp.tanh(log_a) * (a * a + 1.0)) * (i * xc)

    in_group = lax.broadcasted_iota(jnp.int32, (t // SUBLANES, SUBLANES, LRU_W), 1)
    u = u.reshape(t // SUBLANES, SUBLANES, LRU_W)
    a = a.reshape(t // SUBLANES, SUBLANES, LRU_W)
    d = 1
    while d < SUBLANES:
        keep = in_group >= d
        u = a * jnp.where(keep, pltpu.roll(u, d, 1), 0.0) + u
        a = a * jnp.where(keep, pltpu.roll(a, d, 1), 1.0)
        d *= 2
    u = u.reshape(t, LRU_W)
    a = a.reshape(t, LRU_W)
    carry = h_ref[...]
    groups = []
    for g in range(t // SUBLANES):
        rows = slice(g * SUBLANES, (g + 1) * SUBLANES)
        hg = u[rows] + a[rows] * carry
        groups.append(hg)
        carry = hg[SUBLANES - 1:SUBLANES]
    h = jnp.concatenate(groups, axis=0)
    h_ref[...] = carry

    xg = g_ref[0].astype(F32)
    gelu = 0.5 * xg * (1.0 + jnp.tanh(math.sqrt(2.0 / math.pi) * (xg + 0.044715 * (xg * xg * xg))))
    o_ref[0] = (h * gelu).astype(BF16)


def _rglru(proj3, conv_w, conv_b, w_gates, b_gates, lru_lambda):
    b, s, _ = proj3.shape
    t = T_LRU
    assert s % t == 0 and t % TAIL == 0
    gb, xb = OFF_LRU_G // LRU_W, OFF_LRU_X // LRU_W
    kern = functools.partial(_lru_kernel, t=t)
    return pl.pallas_call(
        kern,
        grid=(b, s // t),
        in_specs=[
            pl.BlockSpec((1, t, LRU_W), lambda bi, ti: (bi, ti, gb)),
            pl.BlockSpec((1, t, LRU_W), lambda bi, ti: (bi, ti, xb)),
            pl.BlockSpec((1, TAIL, LRU_W), lambda bi, ti: (bi, jnp.maximum(ti * (t // TAIL) - 1, 0), xb)),
            pl.BlockSpec((CONV_W, LRU_W), lambda bi, ti: (0, 0)),
            pl.BlockSpec((1, LRU_W), lambda bi, ti: (0, 0)),
            pl.BlockSpec((LRU_W, 2 * LRU_W), lambda bi, ti: (0, 0)),
            pl.BlockSpec((1, 2 * LRU_W), lambda bi, ti: (0, 0)),
            pl.BlockSpec((1, LRU_W), lambda bi, ti: (0, 0)),
        ],
        out_specs=pl.BlockSpec((1, t, LRU_W), lambda bi, ti: (bi, ti, 0)),
        out_shape=jax.ShapeDtypeStruct((b, s, LRU_W), BF16),
        scratch_shapes=[pltpu.VMEM((1, LRU_W), F32)],
        compiler_params=pltpu.CompilerParams(
            dimension_semantics=("parallel", "arbitrary"), vmem_limit_bytes=VMEM_LIMIT),
        name="rglru",
    )(proj3, proj3, proj3, conv_w, conv_b, w_gates, b_gates, lru_lambda)


def _gla_kernel(q_ref, k_ref, v_ref, go_ref, lr_ref, wg_ref, bg_ref, nw_ref, o_ref, st_ref, *, t):
    ti = pl.program_id(1)

    @pl.when(ti == 0)
    def _():
        st_ref[...] = jnp.zeros_like(st_ref)

    for bb in range(q_ref.shape[0]):
        c = GLA_CHUNK
        n = t // c
        z = jnp.dot(lr_ref[bb], wg_ref[...], preferred_element_type=F32) + bg_ref[...]
        gk = (jnp.minimum(z, 0.0) - jnp.log1p(jnp.exp(-jnp.abs(z)))) * (1.0 / GLA_NORMALIZER)

        row = lax.broadcasted_iota(jnp.int32, (t, GLA_KW), 0)
        rc = row & (c - 1)
        bc = gk
        d = 1
        while d < c:
            bc = bc + jnp.where(rc >= d, pltpu.roll(bc, d, 0), 0.0)
            d *= 2
        bl3 = jnp.broadcast_to(bc.reshape(n, c, GLA_KW)[:, c - 1:c, :], (n, c, GLA_KW))
        bl = bl3.reshape(t, GLA_KW)

        q = q_ref[bb].astype(F32) * (GLA_DK ** -0.5)
        k = k_ref[bb].astype(F32)
        qe = q * jnp.exp(bc)
        ke = k * jnp.exp(-bc)
        kd = k * jnp.exp(bl - bc)
        dec = jnp.exp(bl)

        lane = lax.broadcasted_iota(jnp.int32, (t, GLA_KW), 1)
        col = lax.broadcasted_iota(jnp.int32, (t, t), 1)
        rowt = lax.broadcasted_iota(jnp.int32, (t, t), 0)
        causal = (col <= rowt) & ((col & -c) == (rowt & -c))
        qe_b = qe.astype(BF16)
        nw = nw_ref[...]
        for h in range(GLA_HEADS):
            in_head = (lane >= h * GLA_DK) & (lane < (h + 1) * GLA_DK)
            ke_h = jnp.where(in_head, ke, 0.0).astype(BF16)
            kd_h = jnp.where(in_head, kd, 0.0).astype(BF16)
            v_h = v_ref[bb, :, h * GLA_DV:(h + 1) * GLA_DV]
            att = lax.dot_general(qe_b, ke_h, (((1,), (1,)), ((), ())), preferred_element_type=F32)
            att = jnp.where(causal, att, 0.0)
            o_intra = jnp.dot(att.astype(BF16), v_h, preferred_element_type=F32)
            st = st_ref[bb, h]
            o_inter = []
            for ci in range(n):
                sl = slice(ci * c, (ci + 1) * c)
                o_inter.append(lax.dot_general(qe_b[sl], st.astype(BF16), (((1,), (1,)), ((), ())),
                                               preferred_element_type=F32))
                kvt = lax.dot_general(v_h[sl], kd_h[sl], (((0,), (0,)), ((), ())),
                                      preferred_element_type=F32)
                st = st * dec[ci * c:ci * c + 1] + kvt
            st_ref[bb, h] = st
            o = o_intra + jnp.concatenate(o_inter, axis=0)
            go = go_ref[bb, :, h * GLA_DV:(h + 1) * GLA_DV].astype(F32)
            o_ref[bb, :, h * GLA_DV:(h + 1) * GLA_DV] = (_rms(o, nw) * (go * jax.nn.sigmoid(go))).astype(BF16)


def _gla(proj3, lr3, w_gup, b_g, norm_w):
    b, s, _ = proj3.shape
    t = T_GLA
    nb = GLA_BATCH_PER_STEP
    assert s % t == 0 and t % GLA_CHUNK == 0 and b % nb == 0
    kern = functools.partial(_gla_kernel, t=t)
    return pl.pallas_call(
        kern,
        grid=(b // nb, s // t),
        in_specs=[
            pl.BlockSpec((nb, t, GLA_KW), lambda bi, ti: (bi, ti, OFF_GQ // GLA_KW)),
            pl.BlockSpec((nb, t, GLA_KW), lambda bi, ti: (bi, ti, OFF_GK // GLA_KW)),
            pl.BlockSpec((nb, t, GLA_VW), lambda bi, ti: (bi, ti, OFF_GV // GLA_VW)),
            pl.BlockSpec((nb, t, GLA_VW), lambda bi, ti: (bi, ti, OFF_GO // GLA_VW)),
            pl.BlockSpec((nb, t, LANES), lambda bi, ti: (bi, ti, 0)),
            pl.BlockSpec((LANES, GLA_KW), lambda bi, ti: (0, 0)),
            pl.BlockSpec((1, GLA_KW), lambda bi, ti: (0, 0)),
            pl.BlockSpec((1, GLA_DV), lambda bi, ti: (0, 0)),
        ],
        out_specs=pl.BlockSpec((nb, t, GLA_VW), lambda bi, ti: (bi, ti, 0)),
        out_shape=jax.ShapeDtypeStruct((b, s, GLA_VW), BF16),
        scratch_shapes=[pltpu.VMEM((nb, GLA_HEADS, GLA_DV, GLA_KW), F32)],
        compiler_params=pltpu.CompilerParams(
            dimension_semantics=("parallel", "arbitrary"), vmem_limit_bytes=VMEM_LIMIT),
        name="gla",
    )(proj3, proj3, proj3, proj3, lr3, w_gup, b_g, norm_w)


def _outproj_kernel(ya_ref, yl_ref, yg_ref, w_ref, x_ref, nw_ref, o_ref):
    y = jnp.concatenate([ya_ref[...], yl_ref[...], yg_ref[...]], axis=1)
    mix = jnp.dot(y, w_ref[...], preferred_element_type=F32)
    o_ref[...] = x_ref[...] + _rms(mix, nw_ref[...])


def _outproj(ya, yl, yg, w_out_b, layer, x2, norm_w):
    m = x2.shape[0]
    tm = TM_OUT
    assert m % tm == 0
    return pl.pallas_call(
        _outproj_kernel,
        grid=(m // tm,),
        in_specs=[
            pl.BlockSpec((tm, DIFF_W), lambda i: (i, 0)),
            pl.BlockSpec((tm, LRU_W), lambda i: (i, 0)),
            pl.BlockSpec((tm, GLA_VW), lambda i: (i, 0)),
            pl.BlockSpec((None, D_MIX, D_MODEL), lambda i: (layer, 0, 0)),
            pl.BlockSpec((tm, D_MODEL), lambda i: (i, 0)),
            pl.BlockSpec((1, D_MODEL), lambda i: (0, 0)),
        ],
        out_specs=pl.BlockSpec((tm, D_MODEL), lambda i: (i, 0)),
        out_shape=jax.ShapeDtypeStruct((m, D_MODEL), F32),
        compiler_params=pltpu.CompilerParams(
            dimension_semantics=("parallel",), vmem_limit_bytes=VMEM_LIMIT),
        name="outproj",
    )(ya, yl, yg, w_out_b, x2, norm_w)


def _ffn_kernel(x_ref, nw1_ref, wg_ref, wu_ref, wd_ref, nw2_ref, o_ref, h_ref, acc_ref):
    f = pl.program_id(1)

    @pl.when(f == 0)
    def _():
        h_ref[...] = _rms(x_ref[...], nw1_ref[...]).astype(BF16)
        acc_ref[...] = jnp.zeros_like(acc_ref)

    h = h_ref[...]
    g = jnp.dot(h, wg_ref[...], preferred_element_type=F32)
    u = jnp.dot(h, wu_ref[...], preferred_element_type=F32)
    hid = (g * jax.nn.sigmoid(g) * u).astype(BF16)
    acc_ref[...] += jnp.dot(hid, wd_ref[...], preferred_element_type=F32)

    @pl.when(f == pl.num_programs(1) - 1)
    def _():
        o_ref[...] = x_ref[...] + _rms(acc_ref[...], nw2_ref[...])


def _ffn(x2, nw1, w_gate_b, w_up_b, w_down_b, layer, nw2):
    m = x2.shape[0]
    tm, tf = TM_FFN, TF_FFN
    assert m % tm == 0 and D_FF % tf == 0
    return pl.pallas_call(
        _ffn_kernel,
        grid=(m // tm, D_FF // tf),
        in_specs=[
            pl.BlockSpec((tm, D_MODEL), lambda i, f: (i, 0)),
            pl.BlockSpec((1, D_MODEL), lambda i, f: (0, 0)),
            pl.BlockSpec((None, D_MODEL, tf), lambda i, f: (layer, 0, f)),
            pl.BlockSpec((None, D_MODEL, tf), lambda i, f: (layer, 0, f)),
            pl.BlockSpec((None, tf, D_MODEL), lambda i, f: (layer, f, 0)),
            pl.BlockSpec((1, D_MODEL), lambda i, f: (0, 0)),
        ],
        out_specs=pl.BlockSpec((tm, D_MODEL), lambda i, f: (i, 0)),
        out_shape=jax.ShapeDtypeStruct((m, D_MODEL), F32),
        scratch_shapes=[pltpu.VMEM((tm, D_MODEL), BF16), pltpu.VMEM((tm, D_MODEL), F32)],
        compiler_params=pltpu.CompilerParams(
            dimension_semantics=("parallel", "arbitrary"), vmem_limit_bytes=VMEM_LIMIT),
        name="ffn",
    )(x2, nw1, w_gate_b, w_up_b, w_down_b, nw2)


def _rope_tables(seq):
    half = DIFF_HD // 2
    inv = ROPE_THETA ** (-jnp.arange(0, DIFF_HD, 2, dtype=F32) / DIFF_HD)
    ang = jnp.arange(seq, dtype=F32)[:, None] * inv[None, :]
    reps = LANES // half
    sign = jnp.tile(jnp.concatenate([-jnp.ones((half,), F32), jnp.ones((half,), F32)]), LANES // DIFF_HD)
    return jnp.stack([jnp.tile(jnp.cos(ang), (1, reps)), jnp.tile(jnp.sin(ang), (1, reps)) * sign[None, :]])


def _block_diag(w):
    n, c, d = w.shape
    eye = jnp.eye(n, dtype=w.dtype)
    return (eye[:, None, :, None] * w[:, :, None, :]).reshape(n * c, n * d)


@jax.jit
def _forward(x, pre_mix_norm, post_mix_norm, pre_ffn_norm, post_ffn_norm, w_in, w_out,
             lambda_q1, lambda_k1, lambda_q2, lambda_k2, diff_subln,
             conv_w, conv_b, w_rgate, b_rgate, w_igate, b_igate, lru_lambda,
             w_gla_gate_up, b_gla_gate, gla_norm, w_ffn_gate, w_ffn_up, w_ffn_down):
    b, s, dm = x.shape
    m = b * s
    rope_tab = _rope_tables(s)
    x2 = x.reshape(m, dm)
    row = lambda v: v.reshape(1, -1).astype(F32)
    w_in_b, w_out_b = w_in.astype(BF16), w_out.astype(BF16)
    w_gate_b, w_up_b, w_down_b = w_ffn_gate.astype(BF16), w_ffn_up.astype(BF16), w_ffn_down.astype(BF16)
    for l in range(DEPTH):
        lam_init = 0.8 - 0.6 * math.exp(-0.3 * l)
        w_lr = jnp.pad(w_in[l, :, N_MAIN:], ((0, 0), (0, LANES - GLA_RANK))).astype(BF16)
        proj, lr = _inproj(x2, row(pre_mix_norm[l]), w_in_b, w_lr, l)
        proj3 = proj.reshape(b, s, N_MAIN)
        lr3 = lr.reshape(b, s, LANES)

        lamv = jnp.pad(jnp.stack([lambda_q1[l], lambda_k1[l], lambda_q2[l], lambda_k2[l]]).astype(F32),
                       ((0, 0), (0, LANES - DIFF_HD)))
        y_attn = _diff_attention(proj3, rope_tab, lamv, row(diff_subln[l]), lam_init)

        w_gates = jnp.concatenate([_block_diag(w_rgate[l]), _block_diag(w_igate[l])], axis=1).astype(BF16)
        b_gates = jnp.concatenate([b_rgate[l], b_igate[l]]).reshape(1, -1).astype(F32)
        y_lru = _rglru(proj3, conv_w[l].astype(F32), row(conv_b[l]), w_gates, b_gates, row(lru_lambda[l]))

        w_gup = jnp.pad(w_gla_gate_up[l], ((0, LANES - GLA_RANK), (0, 0))).astype(BF16)
        y_gla = _gla(proj3, lr3, w_gup, row(b_gla_gate[l]), row(gla_norm[l]))

        x2 = _outproj(y_attn.reshape(m, DIFF_W), y_lru.reshape(m, LRU_W), y_gla.reshape(m, GLA_VW),
                      w_out_b, l, x2, row(post_mix_norm[l]))
        x2 = _ffn(x2, row(pre_ffn_norm[l]), w_gate_b, w_up_b, w_down_b, l, row(post_ffn_norm[l]))
    return x2.reshape(b, s, dm)


def kernel(x, pre_mix_norm, post_mix_norm, pre_ffn_norm, post_ffn_norm, w_in, w_out, lambda_q1, lambda_k1, lambda_q2, lambda_k2, diff_subln, conv_w, conv_b, w_rgate, b_rgate, w_igate, b_igate, lru_lambda, w_gla_gate_up, b_gla_gate, gla_norm, w_ffn_gate, w_ffn_up, w_ffn_down):
    return _forward(x, pre_mix_norm, post_mix_norm, pre_ffn_norm, post_ffn_norm, w_in, w_out,
                    lambda_q1, lambda_k1, lambda_q2, lambda_k2, diff_subln,
                    conv_w, conv_b, w_rgate, b_rgate, w_igate, b_igate, lru_lambda,
                    w_gla_gate_up, b_gla_gate, gla_norm, w_ffn_gate, w_ffn_up, w_ffn_down)
```

```python
import functools
import math

import jax
import jax.numpy as jnp
from jax import lax
from jax.experimental import pallas as pl
from jax.experimental.pallas import tpu as pltpu

F32 = jnp.float32
BF16 = jnp.bfloat16

D_MODEL = 2048
DEPTH = 4
DIFF_HEADS = 8
DIFF_HD = 64
DIFF_VD = 2 * DIFF_HD
DIFF_W = DIFF_HEADS * DIFF_VD
ROPE_THETA = 10000.0
LRU_W = 512
LRU_BLOCKS = 8
CONV_W = 4
LRU_C = 8.0
GLA_HEADS = 4
GLA_DK = 64
GLA_DV = 128
GLA_KW = GLA_HEADS * GLA_DK
GLA_VW = GLA_HEADS * GLA_DV
GLA_RANK = 16
GLA_NORMALIZER = 16.0
GLA_CHUNK = 64
D_MIX = DIFF_W + LRU_W + GLA_VW
N_MAIN = 2 * DIFF_W + DIFF_W + 2 * LRU_W + 2 * GLA_KW + 2 * GLA_VW
D_FF = 5632
NORM_EPS = 1e-6

LANES = 128
SUBLANES = 8
VMEM_LIMIT = 56 * 1024 * 1024

OFF_Q, OFF_K, OFF_V = 0, DIFF_W, 2 * DIFF_W
OFF_LRU_G, OFF_LRU_X = 3 * DIFF_W, 3 * DIFF_W + LRU_W
OFF_GQ = OFF_LRU_X + LRU_W
OFF_GK = OFF_GQ + GLA_KW
OFF_GV = OFF_GK + GLA_KW
OFF_GO = OFF_GV + GLA_VW

TM_PROJ = 1024
TN_PROJ = 512
TQ_ATTN = 512
ATTN_HEADS_PER_STEP = 2
Q_SCALE = DIFF_HD ** -0.5 * math.log2(math.e)
T_LRU = 1024
LRU_BATCH_PER_STEP = 2
T_GLA = 256
GLA_BATCH_PER_STEP = 4
TM_OUT = 512
TM_FFN = 512
TF_FFN = 512
TAIL = 16


def _rms(x, w):
    ms = jnp.mean(x * x, axis=-1, keepdims=True)
    return x * lax.rsqrt(ms + NORM_EPS) * w


def _inproj_kernel(x_ref, nw_ref, w_ref, wlr_ref, o_ref, lr_ref, h_ref):
    @pl.when(pl.program_id(1) == 0)
    def _():
        hb = _rms(x_ref[...], nw_ref[...]).astype(BF16)
        h_ref[...] = hb
        lr_ref[...] = jnp.dot(hb, wlr_ref[...], preferred_element_type=F32).astype(BF16)

    h = h_ref[...]
    half = o_ref.shape[1] // 2
    for c in range(2):
        cols = slice(c * half, (c + 1) * half)
        o_ref[:, cols] = jnp.dot(h, w_ref[:, cols], preferred_element_type=F32).astype(BF16)


def _inproj(x2, norm_w, w_in_b, w_lr, layer):
    m = x2.shape[0]
    tm, tn = TM_PROJ, TN_PROJ
    assert m % tm == 0 and N_MAIN % tn == 0
    return pl.pallas_call(
        _inproj_kernel,
        grid=(m // tm, N_MAIN // tn),
        in_specs=[
            pl.BlockSpec((tm, D_MODEL), lambda i, j: (i, 0)),
            pl.BlockSpec((1, D_MODEL), lambda i, j: (0, 0)),
            pl.BlockSpec((None, D_MODEL, tn), lambda i, j: (layer, 0, j)),
            pl.BlockSpec((D_MODEL, LANES), lambda i, j: (0, 0)),
        ],
        out_specs=[
            pl.BlockSpec((tm, tn), lambda i, j: (i, j)),
            pl.BlockSpec((tm, LANES), lambda i, j: (i, 0)),
        ],
        out_shape=[
            jax.ShapeDtypeStruct((m, N_MAIN), BF16),
            jax.ShapeDtypeStruct((m, LANES), BF16),
        ],
        scratch_shapes=[pltpu.VMEM((tm, D_MODEL), BF16)],
        compiler_params=pltpu.CompilerParams(
            dimension_semantics=("parallel", "arbitrary"), vmem_limit_bytes=VMEM_LIMIT),
        name="inproj",
    )(x2, norm_w, w_in_b, w_lr)


def _rope(a, cos, sin):
    lane = lax.broadcasted_iota(jnp.int32, a.shape, 1)
    lower_half = (lane & (DIFF_HD // 2)) == 0
    rot = jnp.where(lower_half,
                    pltpu.roll(a, LANES - DIFF_HD // 2, 1),
                    pltpu.roll(a, DIFF_HD // 2, 1))
    return a * cos + rot * sin


def _stack_maps(q):
    lane = lax.broadcasted_iota(jnp.int32, q.shape, 1)
    stacked = jnp.concatenate([jnp.where(lane < DIFF_HD, q, 0.0),
                               jnp.where(lane >= DIFF_HD, q, 0.0)], axis=0)
    return stacked.T.astype(BF16)


def _attn_kernel(q_ref, k_ref, v_ref, tab_ref, lamv_ref, sw_ref, o_ref,
                 kr_ref, vt_ref, sa_ref, sb_ref, qq_ref, qn_ref, m_ref, l_ref, acc_ref,
                 *, tq, tk, lam_init):
    nq = q_ref.shape[1] // tq
    heads = range(ATTN_HEADS_PER_STEP)

    def head_lanes(h):
        return slice(h * LANES, (h + 1) * LANES)

    def rotated_q(h, tile):
        rows = pl.ds(pl.multiple_of(tile * tq, tq), tq)
        q = _rope(q_ref[0, rows, head_lanes(h)].astype(F32), tab_ref[0, rows, :], tab_ref[1, rows, :])
        return _stack_maps(q * Q_SCALE)

    def scores(h, qq, blk, s_ref):
        kb = kr_ref[h, pl.ds(pl.multiple_of(blk * tk, tk), tk), :]
        s_ref[h] = jnp.dot(kb, qq, preferred_element_type=F32)

    def process(h, s_ref, blk, mask):
        s = s_ref[h]
        if mask is not None:
            s = jnp.where(mask, s, -jnp.inf)
        m = m_ref[h]
        m_new = jnp.maximum(m, jnp.max(s, axis=0, keepdims=True))
        alpha = jnp.exp2(m - m_new)
        m_ref[h] = m_new
        p = jnp.exp2(s - m_new).astype(BF16)
        pv = jnp.dot(vt_ref[h, blk], p, preferred_element_type=F32)
        l_ref[h] = alpha * l_ref[h] + pv[DIFF_VD:DIFF_VD + 1]
        acc_ref[h] = alpha * acc_ref[h] + pv[:DIFF_VD]

    ones = jnp.ones((16, tk), BF16)
    for h in heads:
        kr_ref[h] = _rope(k_ref[0, :, head_lanes(h)].astype(F32), tab_ref[0], tab_ref[1]).astype(BF16)
        for j in range(vt_ref.shape[1]):
            vt = v_ref[0, j * tk:(j + 1) * tk, head_lanes(h)].astype(F32).T.astype(BF16)
            vt_ref[h, j] = jnp.concatenate([vt, ones], axis=0)
        qn_ref[h] = rotated_q(h, 0)
        scores(h, qn_ref[h], 0, sa_ref)

    lv = lamv_ref[...]
    lam = (jnp.exp(jnp.sum(lv[0:1] * lv[1:2], axis=-1, keepdims=True))
           - jnp.exp(jnp.sum(lv[2:3] * lv[3:4], axis=-1, keepdims=True)) + lam_init)

    def pair(p):
        for h in heads:
            scores(h, qq_ref[h], 2 * p + 1, sb_ref)
        for h in heads:
            process(h, sa_ref, 2 * p, None)
        for h in heads:
            scores(h, qq_ref[h], 2 * p + 2, sa_ref)
        for h in heads:
            process(h, sb_ref, 2 * p + 1, None)

    def two_pairs(j, first):
        pair(first + 2 * j)
        pair(first + 2 * j + 1)
        return first

    def tile(qi, carry):
        qq_ref[...] = qn_ref[...]
        m_ref[...] = jnp.full_like(m_ref, -jnp.inf)
        l_ref[...] = jnp.zeros_like(l_ref)
        acc_ref[...] = jnp.zeros_like(acc_ref)
        odd = qi & 1

        @pl.when(odd == 1)
        def _():
            pair(0)

        lax.fori_loop(0, qi // 2, two_pairs, odd)

        key = lax.broadcasted_iota(jnp.int32, (tk, 2 * tq), 0)
        qry = lax.broadcasted_iota(jnp.int32, (tk, 2 * tq), 1) & (tq - 1)
        d0 = 2 * qi
        for h in heads:
            scores(h, qq_ref[h], d0 + 1, sb_ref)
        for h in heads:
            process(h, sa_ref, d0, key <= qry)
        for h in heads:
            qn_ref[h] = rotated_q(h, jnp.minimum(qi + 1, nq - 1))
            scores(h, qn_ref[h], 0, sa_ref)
        for h in heads:
            process(h, sb_ref, d0 + 1, key + tk <= qry)

        for h in heads:
            o = acc_ref[h] / l_ref[h]
            d = o[:, :tq] - lam * o[:, tq:]
            ms = jnp.mean(d * d, axis=0, keepdims=True)
            y = (d * lax.rsqrt(ms + NORM_EPS)).T * (sw_ref[...] * (1.0 - lam_init))
            o_ref[0, pl.ds(pl.multiple_of(qi * tq, tq), tq), head_lanes(h)] = y.astype(BF16)
        return carry

    lax.fori_loop(0, nq, tile, 0)


def _diff_attention(proj3, rope_tab, lamv, subln_w, lam_init):
    b, s, _ = proj3.shape
    tq = TQ_ATTN
    tk = tq // 2
    hp = ATTN_HEADS_PER_STEP
    w = hp * LANES
    assert s % tq == 0 and tq & (tq - 1) == 0 and DIFF_HEADS % hp == 0
    kb0, vb0 = OFF_K // w, OFF_V // w
    kern = functools.partial(_attn_kernel, tq=tq, tk=tk, lam_init=lam_init)
    return pl.pallas_call(
        kern,
        grid=(b, DIFF_HEADS // hp),
        in_specs=[
            pl.BlockSpec((1, s, w), lambda bi, h: (bi, 0, h)),
            pl.BlockSpec((1, s, w), lambda bi, h: (bi, 0, kb0 + h)),
            pl.BlockSpec((1, s, w), lambda bi, h: (bi, 0, vb0 + h)),
            pl.BlockSpec((2, s, LANES), lambda bi, h: (0, 0, 0)),
            pl.BlockSpec((4, LANES), lambda bi, h: (0, 0)),
            pl.BlockSpec((1, DIFF_VD), lambda bi, h: (0, 0)),
        ],
        out_specs=pl.BlockSpec((1, s, w), lambda bi, h: (bi, 0, h)),
        out_shape=jax.ShapeDtypeStruct((b, s, DIFF_W), BF16),
        scratch_shapes=[
            pltpu.VMEM((hp, s, LANES), BF16), pltpu.VMEM((hp, s // tk, DIFF_VD + 16, tk), BF16),
            pltpu.VMEM((hp, tk, 2 * tq), F32), pltpu.VMEM((hp, tk, 2 * tq), F32),
            pltpu.VMEM((hp, LANES, 2 * tq), BF16), pltpu.VMEM((hp, LANES, 2 * tq), BF16),
            pltpu.VMEM((hp, 1, 2 * tq), F32), pltpu.VMEM((hp, 1, 2 * tq), F32),
            pltpu.VMEM((hp, DIFF_VD, 2 * tq), F32),
        ],
        compiler_params=pltpu.CompilerParams(
            dimension_semantics=("parallel", "parallel"), vmem_limit_bytes=VMEM_LIMIT),
        name="diff_attn",
    )(proj3, proj3, proj3, rope_tab, lamv, subln_w)


def _lru_kernel(g_ref, x_ref, tail_ref, cw_ref, cb_ref, wg_ref, bg_ref, lam_ref, o_ref, h_ref, *, t):
    ti = pl.program_id(1)

    @pl.when(ti == 0)
    def _():
        h_ref[...] = jnp.zeros_like(h_ref)

    for bb in range(x_ref.shape[0]):
        xr = x_ref[bb].astype(F32)
        tail = jnp.where(ti > 0, tail_ref[bb].astype(F32), 0.0)
        ext = jnp.concatenate([tail, xr], axis=0)
        cw = cw_ref[...]
        xc = cb_ref[...] + cw[CONV_W - 1:CONV_W] * xr
        for d in range(1, CONV_W):
            xc = xc + cw[CONV_W - 1 - d:CONV_W - d] * pltpu.roll(ext, d, 0)[TAIL:]

        gates = jnp.dot(xc.astype(BF16), wg_ref[...], preferred_element_type=F32) + bg_ref[...]
        r = jax.nn.sigmoid(gates[:, :LRU_W])
        i = jax.nn.sigmoid(gates[:, LRU_W:])
        lam = lam_ref[...]
        softplus_neg_lam = jnp.maximum(-lam, 0.0) + jnp.log1p(jnp.exp(-jnp.abs(lam)))
        log_a = -LRU_C * r * softplus_neg_lam
        a = jnp.exp(log_a)
        u = jnp.sqrt(-jnp.tanh(log_a) * (a * a + 1.0)) * (i * xc)

        in_group = lax.broadcasted_iota(jnp.int32, (t // SUBLANES, SUBLANES, LRU_W), 1)
        u = u.reshape(t // SUBLANES, SUBLANES, LRU_W)
        a = a.reshape(t // SUBLANES, SUBLANES, LRU_W)
        d = 1
        while d < SUBLANES:
            keep = in_group >= d
            u = a * jnp.where(keep, pltpu.roll(u, d, 1), 0.0) + u
            a = a * jnp.where(keep, pltpu.roll(a, d, 1), 1.0)
            d *= 2
        u = u.reshape(t, LRU_W)
        a = a.reshape(t, LRU_W)
        carry = h_ref[bb]
        groups = []
        for g in range(t // SUBLANES):
            rows = slice(g * SUBLANES, (g + 1) * SUBLANES)
            hg = u[rows] + a[rows] * carry
            groups.append(hg)
            carry = hg[SUBLANES - 1:SUBLANES]
        h = jnp.concatenate(groups, axis=0)
        h_ref[bb] = carry

        xg = g_ref[bb].astype(F32)
        gelu = 0.5 * xg * (1.0 + jnp.tanh(math.sqrt(2.0 / math.pi) * (xg + 0.044715 * (xg * xg * xg))))
        o_ref[bb] = (h * gelu).astype(BF16)


def _rglru(proj3, conv_w, conv_b, w_gates, b_gates, lru_lambda):
    b, s, _ = proj3.shape
    t = T_LRU
    nb = LRU_BATCH_PER_STEP
    assert s % t == 0 and t % TAIL == 0 and b % nb == 0
    gb, xb = OFF_LRU_G // LRU_W, OFF_LRU_X // LRU_W
    kern = functools.partial(_lru_kernel, t=t)
    return pl.pallas_call(
        kern,
        grid=(b // nb, s // t),
        in_specs=[
            pl.BlockSpec((nb, t, LRU_W), lambda bi, ti: (bi, ti, gb)),
            pl.BlockSpec((nb, t, LRU_W), lambda bi, ti: (bi, ti, xb)),
            pl.BlockSpec((nb, TAIL, LRU_W), lambda bi, ti: (bi, jnp.maximum(ti * (t // TAIL) - 1, 0), xb)),
            pl.BlockSpec((CONV_W, LRU_W), lambda bi, ti: (0, 0)),
            pl.BlockSpec((1, LRU_W), lambda bi, ti: (0, 0)),
            pl.BlockSpec((LRU_W, 2 * LRU_W), lambda bi, ti: (0, 0)),
            pl.BlockSpec((1, 2 * LRU_W), lambda bi, ti: (0, 0)),
            pl.BlockSpec((1, LRU_W), lambda bi, ti: (0, 0)),
        ],
        out_specs=pl.BlockSpec((nb, t, LRU_W), lambda bi, ti: (bi, ti, 0)),
        out_shape=jax.ShapeDtypeStruct((b, s, LRU_W), BF16),
        scratch_shapes=[pltpu.VMEM((nb, 1, LRU_W), F32)],
        compiler_params=pltpu.CompilerParams(
            dimension_semantics=("parallel", "arbitrary"), vmem_limit_bytes=VMEM_LIMIT),
        name="rglru",
    )(proj3, proj3, proj3, conv_w, conv_b, w_gates, b_gates, lru_lambda)


def _gla_kernel(q_ref, k_ref, v_ref, go_ref, lr_ref, wg_ref, bg_ref, nw_ref, o_ref, st_ref, *, t):
    ti = pl.program_id(1)

    @pl.when(ti == 0)
    def _():
        st_ref[...] = jnp.zeros_like(st_ref)

    for bb in range(q_ref.shape[0]):
        c = GLA_CHUNK
        n = t // c
        z = jnp.dot(lr_ref[bb], wg_ref[...], preferred_element_type=F32) + bg_ref[...]
        gk = (jnp.minimum(z, 0.0) - jnp.log1p(jnp.exp(-jnp.abs(z)))) * (1.0 / GLA_NORMALIZER)

        row = lax.broadcasted_iota(jnp.int32, (t, GLA_KW), 0)
        rc = row & (c - 1)
        bc = gk
        d = 1
        while d < c:
            bc = bc + jnp.where(rc >= d, pltpu.roll(bc, d, 0), 0.0)
            d *= 2
        bl3 = jnp.broadcast_to(bc.reshape(n, c, GLA_KW)[:, c - 1:c, :], (n, c, GLA_KW))
        bl = bl3.reshape(t, GLA_KW)

        q = q_ref[bb].astype(F32) * (GLA_DK ** -0.5)
        k = k_ref[bb].astype(F32)
        qe = q * jnp.exp(bc)
        ke = k * jnp.exp(-bc)
        kd = k * jnp.exp(bl - bc)
        dec = jnp.exp(bl)

        lane = lax.broadcasted_iota(jnp.int32, (t, GLA_KW), 1)
        col = lax.broadcasted_iota(jnp.int32, (t, t), 1)
        rowt = lax.broadcasted_iota(jnp.int32, (t, t), 0)
        causal = (col <= rowt) & ((col & -c) == (rowt & -c))
        qe_b = qe.astype(BF16)
        nw = nw_ref[...]
        for h in range(GLA_HEADS):
            in_head = (lane >= h * GLA_DK) & (lane < (h + 1) * GLA_DK)
            ke_h = jnp.where(in_head, ke, 0.0).astype(BF16)
            kd_h = jnp.where(in_head, kd, 0.0).astype(BF16)
            v_h = v_ref[bb, :, h * GLA_DV:(h + 1) * GLA_DV]
            att = lax.dot_general(qe_b, ke_h, (((1,), (1,)), ((), ())), preferred_element_type=F32)
            att = jnp.where(causal, att, 0.0)
            o_intra = jnp.dot(att.astype(BF16), v_h, preferred_element_type=F32)
            st = st_ref[bb, h]
            o_inter = []
            for ci in range(n):
                sl = slice(ci * c, (ci + 1) * c)
                o_inter.append(lax.dot_general(qe_b[sl], st.astype(BF16), (((1,), (1,)), ((), ())),
                                               preferred_element_type=F32))
                kvt = lax.dot_general(v_h[sl], kd_h[sl], (((0,), (0,)), ((), ())),
                                      preferred_element_type=F32)
                st = st * dec[ci * c:ci * c + 1] + kvt
            st_ref[bb, h] = st
            o = o_intra + jnp.concatenate(o_inter, axis=0)
            go = go_ref[bb, :, h * GLA_DV:(h + 1) * GLA_DV].astype(F32)
            o_ref[bb, :, h * GLA_DV:(h + 1) * GLA_DV] = (_rms(o, nw) * (go * jax.nn.sigmoid(go))).astype(BF16)


def _gla(proj3, lr3, w_gup, b_g, norm_w):
    b, s, _ = proj3.shape
    t = T_GLA
    nb = GLA_BATCH_PER_STEP
    assert s % t == 0 and t % GLA_CHUNK == 0 and b % nb == 0
    kern = functools.partial(_gla_kernel, t=t)
    return pl.pallas_call(
        kern,
        grid=(b // nb, s // t),
        in_specs=[
            pl.BlockSpec((nb, t, GLA_KW), lambda bi, ti: (bi, ti, OFF_GQ // GLA_KW)),
            pl.BlockSpec((nb, t, GLA_KW), lambda bi, ti: (bi, ti, OFF_GK // GLA_KW)),
            pl.BlockSpec((nb, t, GLA_VW), lambda bi, ti: (bi, ti, OFF_GV // GLA_VW)),
            pl.BlockSpec((nb, t, GLA_VW), lambda bi, ti: (bi, ti, OFF_GO // GLA_VW)),
            pl.BlockSpec((nb, t, LANES), lambda bi, ti: (bi, ti, 0)),
            pl.BlockSpec((LANES, GLA_KW), lambda bi, ti: (0, 0)),
            pl.BlockSpec((1, GLA_KW), lambda bi, ti: (0, 0)),
            pl.BlockSpec((1, GLA_DV), lambda bi, ti: (0, 0)),
        ],
        out_specs=pl.BlockSpec((nb, t, GLA_VW), lambda bi, ti: (bi, ti, 0)),
        out_shape=jax.ShapeDtypeStruct((b, s, GLA_VW), BF16),
        scratch_shapes=[pltpu.VMEM((nb, GLA_HEADS, GLA_DV, GLA_KW), F32)],
        compiler_params=pltpu.CompilerParams(
            dimension_semantics=("parallel", "arbitrary"), vmem_limit_bytes=VMEM_LIMIT),
        name="gla",
    )(proj3, proj3, proj3, proj3, lr3, w_gup, b_g, norm_w)


def _outproj_kernel(ya_ref, yl_ref, yg_ref, w_ref, x_ref, nw_ref, o_ref):
    y = jnp.concatenate([ya_ref[...], yl_ref[...], yg_ref[...]], axis=1)
    mix = jnp.dot(y, w_ref[...], preferred_element_type=F32)
    o_ref[...] = x_ref[...] + _rms(mix, nw_ref[...])


def _outproj(ya, yl, yg, w_out_b, layer, x2, norm_w):
    m = x2.shape[0]
    tm = TM_OUT
    assert m % tm == 0
    return pl.pallas_call(
        _outproj_kernel,
        grid=(m // tm,),
        in_specs=[
            pl.BlockSpec((tm, DIFF_W), lambda i: (i, 0)),
            pl.BlockSpec((tm, LRU_W), lambda i: (i, 0)),
            pl.BlockSpec((tm, GLA_VW), lambda i: (i, 0)),
            pl.BlockSpec((None, D_MIX, D_MODEL), lambda i: (layer, 0, 0)),
            pl.BlockSpec((tm, D_MODEL), lambda i: (i, 0)),
            pl.BlockSpec((1, D_MODEL), lambda i: (0, 0)),
        ],
        out_specs=pl.BlockSpec((tm, D_MODEL), lambda i: (i, 0)),
        out_shape=jax.ShapeDtypeStruct((m, D_MODEL), F32),
        compiler_params=pltpu.CompilerParams(
            dimension_semantics=("parallel",), vmem_limit_bytes=VMEM_LIMIT),
        name="outproj",
    )(ya, yl, yg, w_out_b, x2, norm_w)


def _ffn_kernel(x_ref, nw1_ref, wg_ref, wu_ref, wd_ref, nw2_ref, o_ref, h_ref, acc_ref):
    f = pl.program_id(1)

    @pl.when(f == 0)
    def _():
        h_ref[...] = _rms(x_ref[...], nw1_ref[...]).astype(BF16)
        acc_ref[...] = jnp.zeros_like(acc_ref)

    h = h_ref[...]
    g = jnp.dot(h, wg_ref[...], preferred_element_type=F32)
    u = jnp.dot(h, wu_ref[...], preferred_element_type=F32)
    hid = (g * jax.nn.sigmoid(g) * u).astype(BF16)
    acc_ref[...] += jnp.dot(hid, wd_ref[...], preferred_element_type=F32)

    @pl.when(f == pl.num_programs(1) - 1)
    def _():
        o_ref[...] = x_ref[...] + _rms(acc_ref[...], nw2_ref[...])


def _ffn(x2, nw1, w_gate_b, w_up_b, w_down_b, layer, nw2):
    m = x2.shape[0]
    tm, tf = TM_FFN, TF_FFN
    assert m % tm == 0 and D_FF % tf == 0
    return pl.pallas_call(
        _ffn_kernel,
        grid=(m // tm, D_FF // tf),
        in_specs=[
            pl.BlockSpec((tm, D_MODEL), lambda i, f: (i, 0)),
            pl.BlockSpec((1, D_MODEL), lambda i, f: (0, 0)),
            pl.BlockSpec((None, D_MODEL, tf), lambda i, f: (layer, 0, f)),
            pl.BlockSpec((None, D_MODEL, tf), lambda i, f: (layer, 0, f)),
            pl.BlockSpec((None, tf, D_MODEL), lambda i, f: (layer, f, 0)),
            pl.BlockSpec((1, D_MODEL), lambda i, f: (0, 0)),
        ],
        out_specs=pl.BlockSpec((tm, D_MODEL), lambda i, f: (i, 0)),
        out_shape=jax.ShapeDtypeStruct((m, D_MODEL), F32),
        scratch_shapes=[pltpu.VMEM((tm, D_MODEL), BF16), pltpu.VMEM((tm, D_MODEL), F32)],
        compiler_params=pltpu.CompilerParams(
            dimension_semantics=("parallel", "arbitrary"), vmem_limit_bytes=VMEM_LIMIT),
        name="ffn",
    )(x2, nw1, w_gate_b, w_up_b, w_down_b, nw2)


def _rope_tables(seq):
    half = DIFF_HD // 2
    inv = ROPE_THETA ** (-jnp.arange(0, DIFF_HD, 2, dtype=F32) / DIFF_HD)
    ang = jnp.arange(seq, dtype=F32)[:, None] * inv[None, :]
    reps = LANES // half
    sign = jnp.tile(jnp.concatenate([-jnp.ones((half,), F32), jnp.ones((half,), F32)]), LANES // DIFF_HD)
    return jnp.stack([jnp.tile(jnp.cos(ang), (1, reps)), jnp.tile(jnp.sin(ang), (1, reps)) * sign[None, :]])


def _block_diag(w):
    n, c, d = w.shape
    eye = jnp.eye(n, dtype=w.dtype)
    return (eye[:, None, :, None] * w[:, :, None, :]).reshape(n * c, n * d)


@jax.jit
def _forward(x, pre_mix_norm, post_mix_norm, pre_ffn_norm, post_ffn_norm, w_in, w_out,
             lambda_q1, lambda_k1, lambda_q2, lambda_k2, diff_subln,
             conv_w, conv_b, w_rgate, b_rgate, w_igate, b_igate, lru_lambda,
             w_gla_gate_up, b_gla_gate, gla_norm, w_ffn_gate, w_ffn_up, w_ffn_down):
    b, s, dm = x.shape
    m = b * s
    rope_tab = _rope_tables(s)
    x2 = x.reshape(m, dm)
    row = lambda v: v.reshape(1, -1).astype(F32)
    w_in_b, w_out_b = w_in.astype(BF16), w_out.astype(BF16)
    w_gate_b, w_up_b, w_down_b = w_ffn_gate.astype(BF16), w_ffn_up.astype(BF16), w_ffn_down.astype(BF16)
    for l in range(DEPTH):
        lam_init = 0.8 - 0.6 * math.exp(-0.3 * l)
        w_lr = jnp.pad(w_in[l, :, N_MAIN:], ((0, 0), (0, LANES - GLA_RANK))).astype(BF16)
        proj, lr = _inproj(x2, row(pre_mix_norm[l]), w_in_b, w_lr, l)
        proj3 = proj.reshape(b, s, N_MAIN)
        lr3 = lr.reshape(b, s, LANES)

        lamv = jnp.pad(jnp.stack([lambda_q1[l], lambda_k1[l], lambda_q2[l], lambda_k2[l]]).astype(F32),
                       ((0, 0), (0, LANES - DIFF_HD)))
        y_attn = _diff_attention(proj3, rope_tab, lamv, row(diff_subln[l]), lam_init)

        w_gates = jnp.concatenate([_block_diag(w_rgate[l]), _block_diag(w_igate[l])], axis=1).astype(BF16)
        b_gates = jnp.concatenate([b_rgate[l], b_igate[l]]).reshape(1, -1).astype(F32)
        y_lru = _rglru(proj3, conv_w[l].astype(F32), row(conv_b[l]), w_gates, b_gates, row(lru_lambda[l]))

        w_gup = jnp.pad(w_gla_gate_up[l], ((0, LANES - GLA_RANK), (0, 0))).astype(BF16)
        y_gla = _gla(proj3, lr3, w_gup, row(b_gla_gate[l]), row(gla_norm[l]))

        x2 = _outproj(y_attn.reshape(m, DIFF_W), y_lru.reshape(m, LRU_W), y_gla.reshape(m, GLA_VW),
                      w_out_b, l, x2, row(post_mix_norm[l]))
        x2 = _ffn(x2, row(pre_ffn_norm[l]), w_gate_b, w_up_b, w_down_b, l, row(post_ffn_norm[l]))
    return x2.reshape(b, s, dm)


def kernel(x, pre_mix_norm, post_mix_norm, pre_ffn_norm, post_ffn_norm, w_in, w_out, lambda_q1, lambda_k1, lambda_q2, lambda_k2, diff_subln, conv_w, conv_b, w_rgate, b_rgate, w_igate, b_igate, lru_lambda, w_gla_gate_up, b_gla_gate, gla_norm, w_ffn_gate, w_ffn_up, w_ffn_down):
    return _forward(x, pre_mix_norm, post_mix_norm, pre_ffn_norm, post_ffn_norm, w_in, w_out,
                    lambda_q1, lambda_k1, lambda_q2, lambda_k2, diff_subln,
                    conv_w, conv_b, w_rgate, b_rgate, w_igate, b_igate, lru_lambda,
                    w_gla_gate_up, b_gla_gate, gla_norm, w_ffn_gate, w_ffn_up, w_ffn_down)
```
